```python
import jax
import jax.numpy as jnp
from jax import lax
import numpy as np

D_MODEL = 1024
BATCH = 32
SEQ = 2048
DEPTH = 4
DEC_BATCH = 8
DEC_SEQ = 32
PAST_LEN = 1024

CHUNK = 64
H_A = 4
DK_A = 128
DV_A = 256
H_B = 4
DK_B = 128
DV_B = 256
ROPE_BASE = 10000.0
FFN_DIM = 2816
N_EXPERTS = 8
TOP_K = 2
N_DENSE = (DEPTH + 1) // 2
N_MOE = DEPTH // 2
RMS_EPS = 1e-6
F_FLOOR = 1e-20
N_MOD = 6

SPLIT_SIZES = (H_A * DK_A, H_A * DK_A, H_A * DV_A, H_A * DV_A,
               H_B * DK_B, H_B * DK_B, H_B * DV_B, H_B * DV_B,
               D_MODEL, D_MODEL)
N_IN = sum(SPLIT_SIZES)
SPLIT_POINTS = tuple(int(s) for s in np.cumsum(SPLIT_SIZES)[:-1])

kernel_name = 'hgrn2_retention_gated_hybrid_stream_step'


def rms_norm(x, w):
    xf = x.astype(jnp.float32)
    y = xf * lax.rsqrt(jnp.mean(xf * xf, axis=-1, keepdims=True) + RMS_EPS)
    return (y * w.astype(jnp.float32)).astype(x.dtype)


def rotary(x, pos):
    half = x.shape[-1] // 2
    inv = ROPE_BASE ** (-jnp.arange(half, dtype=jnp.float32) / half)
    ang = pos.astype(jnp.float32)[:, None] * inv[None, :]
    cos = jnp.cos(ang)[None, :, None, :]
    sin = jnp.sin(ang)[None, :, None, :]
    xf = x.astype(jnp.float32)
    x1, x2 = xf[..., :half], xf[..., half:]
    return jnp.concatenate([x1 * cos - x2 * sin, x2 * cos + x1 * sin], axis=-1).astype(x.dtype)


def chunked_linear_recurrence(q, k, v, log_decay, s0):
    bsz, t_len, n_h, _ = q.shape
    d_v = v.shape[-1]
    n_chunks = -(-t_len // CHUNK)
    pad = n_chunks * CHUNK - t_len

    def to_chunks(a):
        a = jnp.pad(a.astype(jnp.float32), ((0, 0), (0, pad), (0, 0), (0, 0)))
        return a.reshape(bsz, n_chunks, CHUNK, n_h, a.shape[-1]).transpose(1, 0, 3, 2, 4)

    qc, kc, vc, gc = (to_chunks(a) for a in (q, k, v, log_decay))
    causal = jnp.tril(jnp.ones((CHUNK, CHUNK), dtype=bool))[:, :, None]
    vector_gate = log_decay.shape[-1] > 1

    def step(state, blk):
        qb, kb, vb, gb = blk
        g_cum = jnp.cumsum(gb, axis=2)
        g_end = g_cum[:, :, -1:, :]
        rel = g_cum[:, :, :, None, :] - g_cum[:, :, None, :, :]
        decay = jnp.where(causal, jnp.exp(jnp.minimum(rel, 0.0)), 0.0)
        if vector_gate:
            scores = jnp.sum(qb[:, :, :, None, :] * kb[:, :, None, :, :] * decay, axis=-1)
        else:
            scores = jnp.einsum('bhtc,bhsc->bhts', qb, kb) * decay[..., 0]
        out = (jnp.einsum('bhts,bhsv->bhtv', scores, vb)
               + jnp.einsum('bhtc,bhcv->bhtv', qb * jnp.exp(g_cum), state))
        new_state = (state * jnp.exp(g_end)[:, :, 0, :, None]
                     + jnp.einsum('bhsc,bhsv->bhcv', kb * jnp.exp(g_end - g_cum), vb))
        return new_state, out

    s_fin, outs = lax.scan(step, s0.astype(jnp.float32), (qc, kc, vc, gc))
    o = outs.transpose(1, 0, 3, 2, 4).reshape(bsz, n_chunks * CHUNK, n_h, d_v)[:, :t_len]
    return o.astype(v.dtype), s_fin.astype(s0.dtype)


def mixer(h, pos, s_a, s_b, w_in_l, lb_l, gna, gnb, wba, wbb, wo):
    bsz, t_len, _ = h.shape
    proj = h @ w_in_l
    q_a, f_a, i_a, og_a, q_b, k_b, v_b, sg_b, mg_a, mg_b = jnp.split(proj, SPLIT_POINTS, axis=-1)

    def heads(a, n):
        return a.reshape(bsz, t_len, n, -1)

    qa = jax.nn.silu(heads(q_a, H_A))
    f_raw = heads(f_a, H_A).astype(jnp.float32)
    lb = lb_l.reshape(H_A, DK_A)
    f_gate = lb + (1.0 - lb) * jax.nn.sigmoid(f_raw)
    log_f = jnp.log(jnp.maximum(f_gate, F_FLOOR))
    ka = ((1.0 - lb) * jax.nn.sigmoid(-f_raw)).astype(h.dtype)
    o_a, s_a_new = chunked_linear_recurrence(qa, ka, heads(i_a, H_A), log_f, s_a)
    o_a = rms_norm(o_a, gna) * jax.nn.sigmoid(heads(og_a, H_A))

    qb = rotary(heads(q_b, H_B), pos)
    kb = rotary(heads(k_b, H_B), pos) * (DK_B ** -0.5)
    log_gamma = jnp.log1p(-jnp.exp2(-5.0 - jnp.arange(H_B, dtype=jnp.float32)))
    g_b = jnp.broadcast_to(log_gamma[:, None], (bsz, t_len, H_B, 1))
    o_b, s_b_new = chunked_linear_recurrence(qb, kb, heads(v_b, H_B), g_b, s_b)
    o_b = rms_norm(o_b, gnb) * jax.nn.silu(heads(sg_b, H_B))

    y_a = o_a.reshape(bsz, t_len, H_A * DV_A) @ wba
    y_b = o_b.reshape(bsz, t_len, H_B * DV_B) @ wbb
    merged = jax.nn.sigmoid(mg_a) * y_a + jax.nn.sigmoid(mg_b) * y_b
    return merged @ wo, s_a_new, s_b_new


def swiglu(h, wg, wu, wd):
    return (jax.nn.silu(h @ wg) * (h @ wu)) @ wd


def moe_ffn(h, w_router, b_router, wg, wu, wd):
    bsz, t_len, d = h.shape
    hf = h.reshape(-1, d)
    logits = (hf @ w_router).astype(jnp.float32) + b_router.astype(jnp.float32)
    top_v, top_i = lax.top_k(logits, TOP_K)
    top_w = jax.nn.softmax(top_v, axis=-1)
    combine = jnp.sum(jax.nn.one_hot(top_i, N_EXPERTS, dtype=jnp.float32) * top_w[..., None], axis=1)
    out = jnp.zeros(hf.shape, jnp.float32)
    for e in range(N_EXPERTS):
        out = out + combine[:, e:e + 1] * swiglu(hf, wg[e], wu[e], wd[e])
    return out.astype(h.dtype).reshape(bsz, t_len, d)


def trunk(x, c, pos, s_a_in, s_b_in, w_mod, b_mod, norm_gains, w_in, lb_logits,
          g_norm_a, g_norm_b, w_branch_a, w_branch_b, w_out,
          w_gate_dense, w_up_dense, w_down_dense,
          w_router, b_router, w_gate_exp, w_up_exp, w_down_exp):
    bsz = x.shape[0]
    lb_p = jax.nn.softmax(lb_logits.astype(jnp.float32), axis=0)
    lower_bounds = jnp.clip(jnp.cumsum(lb_p, axis=0) - lb_p[:1], 0.0, 1.0)
    c_act = jax.nn.silu(c)
    sa_out, sb_out = [], []
    for l in range(DEPTH):
        mod = (c_act @ w_mod[l] + b_mod[l]).reshape(bsz, N_MOD, 1, D_MODEL)
        shift_m, scale_m, gate_m, shift_f, scale_f, gate_f = (mod[:, i] for i in range(N_MOD))
        if s_a_in is None:
            s_a = jnp.zeros((bsz, H_A, DK_A, DV_A), x.dtype)
            s_b = jnp.zeros((bsz, H_B, DK_B, DV_B), x.dtype)
        else:
            s_a, s_b = s_a_in[l], s_b_in[l]
        h = rms_norm(x, norm_gains[l, 0]) * (1.0 + scale_m) + shift_m
        m_out, s_a_new, s_b_new = mixer(h, pos, s_a, s_b, w_in[l], lower_bounds[l],
                                        g_norm_a[l], g_norm_b[l], w_branch_a[l],
                                        w_branch_b[l], w_out[l])
        x = x + gate_m * rms_norm(m_out, norm_gains[l, 1])
        h = rms_norm(x, norm_gains[l, 2]) * (1.0 + scale_f) + shift_f
        j = l // 2
        if l % 2 == 0:
            f_out = swiglu(h, w_gate_dense[j], w_up_dense[j], w_down_dense[j])
        else:
            f_out = moe_ffn(h, w_router[j], b_router[j], w_gate_exp[j], w_up_exp[j], w_down_exp[j])
        x = x + gate_f * rms_norm(f_out, norm_gains[l, 3])
        sa_out.append(s_a_new)
        sb_out.append(s_b_new)
    return x, jnp.stack(sa_out), jnp.stack(sb_out)


def setup_inputs(seed: int = 0) -> dict:
    key = jax.random.key(seed)
    ks = jax.random.split(key, 26)
    f32 = jnp.float32

    def nrm(k, shape, scale):
        return jax.random.normal(k, shape, f32) * scale

    d = D_MODEL
    return {
        'x_prompt': nrm(ks[0], (BATCH, SEQ, d), 1.0),
        'x_sample': nrm(ks[1], (DEC_BATCH, DEC_SEQ, d), 1.0),
        'state_hgrn': nrm(ks[2], (DEPTH, DEC_BATCH, H_A, DK_A, DV_A), 0.5),
        'state_ret': nrm(ks[3], (DEPTH, DEC_BATCH, H_B, DK_B, DV_B), 1.0),
        'c_prompt': nrm(ks[4], (BATCH, d), 1.0),
        'c_sample': nrm(ks[5], (DEC_BATCH, d), 1.0),
        'w_mod': nrm(ks[6], (DEPTH, d, N_MOD * d), 0.5 * d ** -0.5),
        'b_mod': nrm(ks[7], (DEPTH, N_MOD * d), 0.01),
        'norm_gains': 1.0 + nrm(ks[8], (DEPTH, 4, d), 0.05),
        'w_in': nrm(ks[9], (DEPTH, d, N_IN), d ** -0.5),
        'lb_logits': nrm(ks[10], (DEPTH, H_A * DK_A), 0.5),
        'g_norm_a': 1.0 + nrm(ks[11], (DEPTH, H_A, DV_A), 0.05),
        'g_norm_b': 1.0 + nrm(ks[12], (DEPTH, H_B, DV_B), 0.05),
        'w_branch_a': nrm(ks[13], (DEPTH, H_A * DV_A, d), (H_A * DV_A) ** -0.5),
        'w_branch_b': nrm(ks[14], (DEPTH, H_B * DV_B, d), (H_B * DV_B) ** -0.5),
        'w_out': nrm(ks[15], (DEPTH, d, d), d ** -0.5),
        'w_gate_dense': nrm(ks[16], (N_DENSE, d, FFN_DIM), d ** -0.5),
        'w_up_dense': nrm(ks[17], (N_DENSE, d, FFN_DIM), d ** -0.5),
        'w_down_dense': nrm(ks[18], (N_DENSE, FFN_DIM, d), FFN_DIM ** -0.5),
        'w_router': nrm(ks[19], (N_MOE, d, N_EXPERTS), d ** -0.5),
        'b_router': nrm(ks[20], (N_MOE, N_EXPERTS), 0.01),
        'w_gate_exp': nrm(ks[21], (N_MOE, N_EXPERTS, d, FFN_DIM), d ** -0.5),
        'w_up_exp': nrm(ks[22], (N_MOE, N_EXPERTS, d, FFN_DIM), d ** -0.5),
        'w_down_exp': nrm(ks[23], (N_MOE, N_EXPERTS, FFN_DIM, d), FFN_DIM ** -0.5),
    }


def reference(x_prompt, x_sample, state_hgrn, state_ret, c_prompt, c_sample,
              w_mod, b_mod, norm_gains, w_in, lb_logits, g_norm_a, g_norm_b,
              w_branch_a, w_branch_b, w_out, w_gate_dense, w_up_dense, w_down_dense,
              w_router, b_router, w_gate_exp, w_up_exp, w_down_exp):
    weights = (w_mod, b_mod, norm_gains, w_in, lb_logits, g_norm_a, g_norm_b,
               w_branch_a, w_branch_b, w_out, w_gate_dense, w_up_dense, w_down_dense,
               w_router, b_router, w_gate_exp, w_up_exp, w_down_exp)
    pos_prompt = jnp.arange(x_prompt.shape[1], dtype=jnp.int32)
    pos_sample = PAST_LEN + jnp.arange(x_sample.shape[1], dtype=jnp.int32)
    y_prompt, hgrn_prompt, ret_prompt = trunk(x_prompt, c_prompt, pos_prompt, None, None, *weights)
    y_sample, hgrn_sample, ret_sample = trunk(x_sample, c_sample, pos_sample, state_hgrn, state_ret, *weights)
    return (y_prompt, y_sample, hgrn_prompt, ret_prompt, hgrn_sample, ret_sample)
```

```python
import functools
import math

import jax
import jax.numpy as jnp
from jax import lax
from jax.experimental import pallas as pl
from jax.experimental.pallas import tpu as pltpu

F32 = jnp.float32
BF16 = jnp.bfloat16

D_MODEL = 1024
H_A, DK_A, DV_A = 4, 128, 256
H_B, DK_B, DV_B = 4, 128, 256
ROPE_BASE = 10000.0
FFN_DIM = 2816
N_EXPERTS = 8
RMS_EPS = 1e-6
F_FLOOR = 1e-20
N_MOD = 6
N_IN = 8 * D_MODEL
OFF_QA, OFF_FA, OFF_IA, OFF_OGA = 0, 512, 1024, 2048
OFF_QB, OFF_KB, OFF_VB, OFF_SGB = 3072, 3584, 4096, 5120
REC_COLS = 6144
SUB = 16
LOG_GAMMA = tuple(math.log1p(-(2.0 ** (-5.0 - h))) for h in range(H_B))
VMEM_LIMIT = 56 * 1024 * 1024
FFN_CHUNK = 1408


def _params(sem):
    return pltpu.CompilerParams(dimension_semantics=sem, vmem_limit_bytes=VMEM_LIMIT)


def _resident(shape, index_map):
    return pl.BlockSpec(shape, index_map, pipeline_mode=pl.Buffered(1))


def _rms(x, gain):
    return x * lax.rsqrt(jnp.mean(x * x, axis=-1, keepdims=True) + RMS_EPS) * gain


def _sigmoid(x):
    return 1.0 / (1.0 + jnp.exp(-x))


def _dot(a, b):
    return jnp.dot(a, b, preferred_element_type=F32)


def _dot_nt(a, b):
    return lax.dot_general(a, b, (((1,), (1,)), ((), ())), preferred_element_type=F32)


def _dot_tn(a, b):
    return lax.dot_general(a, b, (((0,), (0,)), ((), ())), preferred_element_type=F32)


def _mod_kernel(c_ref, w_ref, b_ref, o_ref):
    c = c_ref[...]
    c_act = (c * _sigmoid(c)).astype(BF16)
    o_ref[...] = _dot(c_act, w_ref[...].astype(BF16)) + b_ref[...]


def _modulation(c_all, w_mod, b_mod):
    depth = w_mod.shape[0]
    n_seq = c_all.shape[0]
    d = D_MODEL
    return pl.pallas_call(
        _mod_kernel,
        grid=(depth, N_MOD),
        in_specs=[
            pl.BlockSpec((n_seq, d), lambda l, j: (0, 0)),
            pl.BlockSpec((None, d, d), lambda l, j: (l, 0, j)),
            pl.BlockSpec((None, None, 1, d), lambda l, j: (l, j, 0, 0)),
        ],
        out_specs=pl.BlockSpec((None, None, n_seq, d), lambda l, j: (l, j, 0, 0)),
        out_shape=jax.ShapeDtypeStruct((depth, N_MOD, n_seq, d), F32),
        compiler_params=_params(("parallel", "parallel")),
        name="modulation",
    )(c_all, w_mod, b_mod.reshape(depth, N_MOD, 1, d))


def _inproj_kernel(x_ref, mod_ref, gain_ref, w_ref, o_ref, *, tn):
    h = _rms(x_ref[...], gain_ref[...]) * (1.0 + mod_ref[1]) + mod_ref[0]
    hb = h.astype(BF16)
    for n in range(N_IN // tn):
        o_ref[:, n * tn:(n + 1) * tn] = _dot(hb, w_ref[:, n * tn:(n + 1) * tn]).astype(BF16)


def _in_projection(x, mod_l, gain, w_in_l, tm):
    bsz, t_len, d = x.shape
    return pl.pallas_call(
        functools.partial(_inproj_kernel, tn=1024),
        grid=(bsz, t_len // tm),
        in_specs=[
            pl.BlockSpec((None, tm, d), lambda b, i: (b, i, 0)),
            pl.BlockSpec((3, None, 1, d), lambda b, i: (0, b, 0, 0)),
            pl.BlockSpec((1, d), lambda b, i: (0, 0)),
            _resident((d, N_IN), lambda b, i: (0, 0)),
        ],
        out_specs=pl.BlockSpec((None, tm, N_IN), lambda b, i: (b, i, 0)),
        out_shape=jax.ShapeDtypeStruct((bsz, t_len, N_IN), BF16),
        compiler_params=_params(("parallel", "parallel")),
        name="in_projection",
    )(x, mod_l, gain, w_in_l)


def _cumsum_rows(g, tril):
    g1 = g.astype(BF16)
    r1 = g - g1.astype(F32)
    g2 = r1.astype(BF16)
    g3 = (r1 - g2.astype(F32)).astype(BF16)
    return _dot(tril, g1) + _dot(tril, g2) + _dot(tril, g3)


def _hgrn_chunk(qa, k, v, g, s_prev, tril, l_a):
    gc = _cumsum_rows(g, tril)
    row = lax.broadcasted_iota(jnp.int32, (SUB, SUB), 0)
    col = lax.broadcasted_iota(jnp.int32, (SUB, SUB), 1)
    o_rows = []
    for i in range(l_a // SUB):
        lo = i * SUB
        g_i = gc[lo:lo + SUB]
        q_i = qa[lo:lo + SUB]
        k_i = k[lo:lo + SUB]
        a_d = jnp.zeros((SUB, SUB), F32)
        for s in range(SUB):
            p = q_i * jnp.exp(jnp.minimum(g_i - g_i[s:s + 1], 0.0)) * k_i[s:s + 1]
            a_col = jnp.sum(p, axis=-1, keepdims=True)
            a_d = jnp.where((col == s) & (row >= s), a_col, a_d)
        o_i = _dot(a_d.astype(BF16), v[lo:lo + SUB])
        if i > 0:
            g_ref = gc[lo - 1:lo]
            q_hat = (q_i * jnp.exp(g_i - g_ref)).astype(BF16)
            k_hat = (k[:lo] * jnp.exp(g_ref - gc[:lo])).astype(BF16)
            a_off = _dot_nt(q_hat, k_hat)
            o_i = o_i + _dot(a_off.astype(BF16), v[:lo])
        o_rows.append(o_i)
    o = jnp.concatenate(o_rows, axis=0) if len(o_rows) > 1 else o_rows[0]
    o = o + _dot((qa * jnp.exp(gc)).astype(BF16), s_prev.astype(BF16))
    g_end = gc[l_a - 1:l_a]
    k_tail = k * jnp.exp(g_end - gc)
    packed = jnp.concatenate(
        [k_tail, jnp.broadcast_to(jnp.exp(g_end), (DK_A - l_a, DK_A))], axis=0)
    packed_t = packed.T
    s_new = s_prev * packed_t[:, l_a:l_a + 1] + _dot(packed_t[:, :l_a].astype(BF16), v)
    return o, s_new


def _recurrence_kernel(proj_ref, cos_ref, sin_ref, lb_ref, gna_ref, gnb_ref, sa0_ref, sb0_ref,
                       oa_ref, ob_ref, sa_out_ref, sb_out_ref, sa_s, sb_s, *, tb, l_a, l_b):
    t = pl.program_id(1)

    @pl.when(t == 0)
    def _():
        sa_s[...] = sa0_ref[...]
        sb_s[...] = sb0_ref[...]

    tril = (lax.broadcasted_iota(jnp.int32, (l_a, l_a), 0)
            >= lax.broadcasted_iota(jnp.int32, (l_a, l_a), 1)).astype(BF16)

    def chunk_a(c, carry):
        r0 = pl.multiple_of(c * l_a, l_a)
        rows = pl.ds(r0, l_a)
        for h in range(H_A):
            q = proj_ref[rows, OFF_QA + h * DK_A:OFF_QA + (h + 1) * DK_A].astype(F32)
            f_raw = proj_ref[rows, OFF_FA + h * DK_A:OFF_FA + (h + 1) * DK_A].astype(F32)
            v = proj_ref[rows, OFF_IA + h * DV_A:OFF_IA + (h + 1) * DV_A]
            og = proj_ref[rows, OFF_OGA + h * DV_A:OFF_OGA + (h + 1) * DV_A].astype(F32)
            lb = lb_ref[h]
            sig = _sigmoid(f_raw)
            g = jnp.log(jnp.maximum(lb + (1.0 - lb) * sig, F_FLOOR))
            k = (1.0 - lb) * (1.0 - sig)
            qa = q * _sigmoid(q)
            o, s_new = _hgrn_chunk(qa, k, v, g, sa_s[h], tril, l_a)
            sa_s[h] = s_new
            o = _rms(o, gna_ref[h]) * _sigmoid(og)
            oa_ref[rows, h * DV_A:(h + 1) * DV_A] = o.astype(BF16)
        return carry

    lax.fori_loop(0, tb // l_a, chunk_a, 0)

    ti = lax.broadcasted_iota(jnp.int32, (l_b, l_b), 0)
    si = lax.broadcasted_iota(jnp.int32, (l_b, l_b), 1)
    rel = (ti - si).astype(F32)
    tcol = lax.broadcasted_iota(jnp.int32, (l_b, 1), 0).astype(F32)

    def chunk_b(c, carry):
        r0 = pl.multiple_of(c * l_b, l_b)
        rows = pl.ds(r0, l_b)
        cos = cos_ref[rows, :]
        sin = sin_ref[rows, :]
        for h in range(H_B):
            lg = LOG_GAMMA[h]
            q = proj_ref[rows, OFF_QB + h * DK_B:OFF_QB + (h + 1) * DK_B].astype(F32)
            k = proj_ref[rows, OFF_KB + h * DK_B:OFF_KB + (h + 1) * DK_B].astype(F32)
            v = proj_ref[rows, OFF_VB + h * DV_B:OFF_VB + (h + 1) * DV_B]
            sg = proj_ref[rows, OFF_SGB + h * DV_B:OFF_SGB + (h + 1) * DV_B].astype(F32)
            q = q * cos + pltpu.roll(q, DK_B // 2, axis=1) * sin
            k = (k * cos + pltpu.roll(k, DK_B // 2, axis=1) * sin) * (DK_B ** -0.5)
            decay = jnp.where(ti >= si, jnp.exp(jnp.minimum(rel * lg, 0.0)), 0.0)
            scores = _dot_nt(q.astype(BF16), k.astype(BF16)) * decay
            s_prev = sb_s[h]
            o = (_dot(scores.astype(BF16), v)
                 + _dot((q * jnp.exp((tcol + 1.0) * lg)).astype(BF16), s_prev.astype(BF16)))
            k_tail = (k * jnp.exp((l_b - 1.0 - tcol) * lg)).astype(BF16)
            sb_s[h] = s_prev * math.exp(l_b * lg) + _dot_tn(k_tail, v)
            o = _rms(o, gnb_ref[h]) * (sg * _sigmoid(sg))
            ob_ref[rows, h * DV_B:(h + 1) * DV_B] = o.astype(BF16)
        return carry

    lax.fori_loop(0, tb // l_b, chunk_b, 0)

    @pl.when(t == pl.num_programs(1) - 1)
    def _():
        sa_out_ref[...] = sa_s[...]
        sb_out_ref[...] = sb_s[...]


def _recurrences(proj, cos_t, sin_t, lb, gna, gnb, s_a0, s_b0, tb):
    bsz, t_len, _ = proj.shape
    l_a = min(64, t_len)
    l_b = min(128, t_len)
    state_spec = pl.BlockSpec((None, H_A, DK_A, DV_A), lambda b, t: (b, 0, 0, 0))
    act_spec = pl.BlockSpec((None, tb, D_MODEL), lambda b, t: (b, t, 0))
    return pl.pallas_call(
        functools.partial(_recurrence_kernel, tb=tb, l_a=l_a, l_b=l_b),
        grid=(bsz, t_len // tb),
        in_specs=[
            pl.BlockSpec((None, tb, REC_COLS), lambda b, t: (b, t, 0)),
            pl.BlockSpec((tb, DK_B), lambda b, t: (t, 0)),
            pl.BlockSpec((tb, DK_B), lambda b, t: (t, 0)),
            pl.BlockSpec((H_A, 1, DK_A), lambda b, t: (0, 0, 0)),
            pl.BlockSpec((H_A, 1, DV_A), lambda b, t: (0, 0, 0)),
            pl.BlockSpec((H_B, 1, DV_B), lambda b, t: (0, 0, 0)),
            state_spec, state_spec,
        ],
        out_specs=[act_spec, act_spec, state_spec, state_spec],
        out_shape=[
            jax.ShapeDtypeStruct((bsz, t_len, D_MODEL), BF16),
            jax.ShapeDtypeStruct((bsz, t_len, D_MODEL), BF16),
            jax.ShapeDtypeStruct((bsz, H_A, DK_A, DV_A), F32),
            jax.ShapeDtypeStruct((bsz, H_B, DK_B, DV_B), F32),
        ],
        scratch_shapes=[pltpu.VMEM((H_A, DK_A, DV_A), F32), pltpu.VMEM((H_B, DK_B, DV_B), F32)],
        compiler_params=_params(("parallel", "arbitrary")),
        name="recurrences",
    )(proj, cos_t, sin_t, lb, gna, gnb, s_a0, s_b0)


def _merge_kernel(*refs, with_router):
    if with_router:
        (oa_ref, ob_ref, mg_ref, x_ref, mod_ref, gains_ref, wba_ref, wbb_ref, wo_ref,
         wr_ref, br_ref, xo_ref, h2_ref, comb_ref) = refs
    else:
        (oa_ref, ob_ref, mg_ref, x_ref, mod_ref, gains_ref, wba_ref, wbb_ref, wo_ref,
         xo_ref, h2_ref) = refs
    d = D_MODEL
    y_a = _dot(oa_ref[...], wba_ref[...])
    y_b = _dot(ob_ref[...], wbb_ref[...])
    merged = (_sigmoid(mg_ref[:, :d].astype(F32)) * y_a
              + _sigmoid(mg_ref[:, d:].astype(F32)) * y_b)
    m_out = _dot(merged.astype(BF16), wo_ref[...])
    x = x_ref[...] + mod_ref[2] * _rms(m_out, gains_ref[1:2])
    xo_ref[...] = x
    h2 = _rms(x, gains_ref[2:3]) * (1.0 + mod_ref[4]) + mod_ref[3]
    h2_ref[...] = h2.astype(BF16)
    if with_router:
        logits = jnp.dot(h2, wr_ref[...], preferred_element_type=F32,
                         precision=lax.Precision.HIGHEST) + br_ref[...]
        lane = lax.broadcasted_iota(jnp.int32, logits.shape, 1)
        m1 = jnp.max(logits, axis=-1, keepdims=True)
        i1 = jnp.min(jnp.where(logits == m1, lane, N_EXPERTS), axis=-1, keepdims=True)
        rest = jnp.where(lane == i1, -jnp.inf, logits)
        m2 = jnp.max(rest, axis=-1, keepdims=True)
        i2 = jnp.min(jnp.where(rest == m2, lane, N_EXPERTS), axis=-1, keepdims=True)
        e2 = jnp.exp(m2 - m1)
        w1 = 1.0 / (1.0 + e2)
        w2 = e2 / (1.0 + e2)
        comb_ref[...] = jnp.where(lane == i1, w1, 0.0) + jnp.where(lane == i2, w2, 0.0)


def _merge(o_a, o_b, proj, x, mod_l, gains_l, wba, wbb, wo, router, tm):
    bsz, t_len, d = x.shape
    with_router = router is not None
    tok = pl.BlockSpec((None, tm, d), lambda b, i: (b, i, 0))
    in_specs = [
        tok, tok,
        pl.BlockSpec((None, tm, 2 * d), lambda b, i: (b, i, 3)),
        tok,
        pl.BlockSpec((N_MOD, None, 1, d), lambda b, i: (0, b, 0, 0)),
        pl.BlockSpec((4, d), lambda b, i: (0, 0)),
        _resident((d, d), lambda b, i: (0, 0)),
        _resident((d, d), lambda b, i: (0, 0)),
        _resident((d, d), lambda b, i: (0, 0)),
    ]
    args = [o_a, o_b, proj, x, mod_l, gains_l, wba, wbb, wo]
    out_specs = [tok, tok]
    out_shape = [jax.ShapeDtypeStruct((bsz, t_len, d), F32),
                 jax.ShapeDtypeStruct((bsz, t_len, d), BF16)]
    if with_router:
        w_r, b_r = router
        in_specs += [pl.BlockSpec((d, N_EXPERTS), lambda b, i: (0, 0)),
                     pl.BlockSpec((1, N_EXPERTS), lambda b, i: (0, 0))]
        args += [w_r, b_r.reshape(1, N_EXPERTS)]
        out_specs.append(pl.BlockSpec((None, tm, N_EXPERTS), lambda b, i: (b, i, 0)))
        out_shape.append(jax.ShapeDtypeStruct((bsz, t_len, N_EXPERTS), F32))
    return pl.pallas_call(
        functools.partial(_merge_kernel, with_router=with_router),
        grid=(bsz, t_len // tm),
        in_specs=in_specs,
        out_specs=out_specs,
        out_shape=out_shape,
        compiler_params=_params(("parallel", "parallel")),
        name="merge_router" if with_router else "merge",
    )(*args)


def _ffn_kernel(*refs, n_exp):
    if n_exp > 1:
        (h_ref, wg_ref, wu_ref, wd_ref, comb_ref, x_ref, mod_ref, gain_ref, o_ref, acc_ref) = refs
    else:
        (h_ref, wg_ref, wu_ref, wd_ref, x_ref, mod_ref, gain_ref, o_ref, acc_ref) = refs
    e = pl.program_id(2)
    f = pl.program_id(3)

    @pl.when((e == 0) & (f == 0))
    def _():
        acc_ref[...] = jnp.zeros_like(acc_ref)

    h = h_ref[...]
    gate = _dot(h, wg_ref[...])
    up = _dot(h, wu_ref[...])
    y = _dot((gate * _sigmoid(gate) * up).astype(BF16), wd_ref[...])
    if n_exp > 1:
        comb = comb_ref[...]
        lane = lax.broadcasted_iota(jnp.int32, comb.shape, 1)
        y = y * jnp.sum(jnp.where(lane == e, comb, 0.0), axis=-1, keepdims=True)
    acc_ref[...] += y

    @pl.when((e == n_exp - 1) & (f == pl.num_programs(3) - 1))
    def _():
        o_ref[...] = x_ref[...] + mod_ref[0] * _rms(acc_ref[...], gain_ref[...])


def _ffn(h2, wg, wu, wd, comb, x, mod_l, gain, tm):
    bsz, t_len, d = x.shape
    n_exp = wg.shape[0]
    n_f = FFN_DIM // FFN_CHUNK
    tok = pl.BlockSpec((None, tm, d), lambda b, i, e, f: (b, i, 0))
    in_specs = [
        tok,
        pl.BlockSpec((None, d, FFN_CHUNK), lambda b, i, e, f: (e, 0, f)),
        pl.BlockSpec((None, d, FFN_CHUNK), lambda b, i, e, f: (e, 0, f)),
        pl.BlockSpec((None, FFN_CHUNK, d), lambda b, i, e, f: (e, f, 0)),
    ]
    args = [h2, wg, wu, wd]
    if n_exp > 1:
        in_specs.append(pl.BlockSpec((None, tm, N_EXPERTS), lambda b, i, e, f: (b, i, 0)))
        args.append(comb)
    in_specs += [
        tok,
        pl.BlockSpec((1, None, 1, d), lambda b, i, e, f: (5, b, 0, 0)),
        pl.BlockSpec((1, d), lambda b, i, e, f: (0, 0)),
    ]
    args += [x, mod_l, gain]
    return pl.pallas_call(
        functools.partial(_ffn_kernel, n_exp=n_exp),
        grid=(bsz, t_len // tm, n_exp, n_f),
        in_specs=in_specs,
        out_specs=tok,
        out_shape=jax.ShapeDtypeStruct((bsz, t_len, d), F32),
        scratch_shapes=[pltpu.VMEM((tm, d), F32)],
        compiler_params=_params(("parallel", "parallel", "arbitrary", "arbitrary")),
        name="ffn_experts" if n_exp > 1 else "ffn_dense",
    )(*args)


def _rope_tables(pos):
    half = DK_B // 2
    inv = ROPE_BASE ** (-jnp.arange(half, dtype=F32) / half)
    ang = pos.astype(F32)[:, None] * inv[None, :]
    cos, sin = jnp.cos(ang), jnp.sin(ang)
    return jnp.concatenate([cos, cos], axis=-1), jnp.concatenate([-sin, sin], axis=-1)


def _trunk(x, mod, pos, s_a_in, s_b_in, lower_bounds, wts, tm, tb):
    depth = mod.shape[0]
    bsz = x.shape[0]
    cos_t, sin_t = _rope_tables(pos)
    sa_out, sb_out = [], []
    for l in range(depth):
        mod_l = mod[l]
        gains_l = wts["norm_gains"][l]
        if s_a_in is None:
            s_a = jnp.zeros((bsz, H_A, DK_A, DV_A), F32)
            s_b = jnp.zeros((bsz, H_B, DK_B, DV_B), F32)
        else:
            s_a, s_b = s_a_in[l], s_b_in[l]
        proj = _in_projection(x, mod_l, gains_l[0:1], wts["w_in"][l], tm)
        o_a, o_b, s_a_new, s_b_new = _recurrences(
            proj, cos_t, sin_t, lower_bounds[l].reshape(H_A, 1, DK_A),
            wts["g_norm_a"][l].reshape(H_A, 1, DV_A), wts["g_norm_b"][l].reshape(H_B, 1, DV_B),
            s_a, s_b, tb)
        j = l // 2
        router = None if l % 2 == 0 else (wts["w_router"][j], wts["b_router"][j])
        outs = _merge(o_a, o_b, proj, x, mod_l, gains_l, wts["w_branch_a"][l],
                      wts["w_branch_b"][l], wts["w_out"][l], router, tm)
        if l % 2 == 0:
            x, h2 = outs
            x = _ffn(h2, wts["w_gate_dense"][j:j + 1], wts["w_up_dense"][j:j + 1],
                     wts["w_down_dense"][j:j + 1], None, x, mod_l, gains_l[3:4], tm)
        else:
            x, h2, comb = outs
            x = _ffn(h2, wts["w_gate_exp"][j], wts["w_up_exp"][j], wts["w_down_exp"][j],
                     comb, x, mod_l, gains_l[3:4], tm)
        sa_out.append(s_a_new)
        sb_out.append(s_b_new)
    return x, jnp.stack(sa_out), jnp.stack(sb_out)


def kernel(x_prompt, x_sample, state_hgrn, state_ret, c_prompt, c_sample, w_mod, b_mod, norm_gains,
           w_in, lb_logits, g_norm_a, g_norm_b, w_branch_a, w_branch_b, w_out, w_gate_dense,
           w_up_dense, w_down_dense, w_router, b_router, w_gate_exp, w_up_exp, w_down_exp):
    n_prompt = x_prompt.shape[0]
    past_len = 1024
    wts = dict(
        norm_gains=norm_gains, g_norm_a=g_norm_a, g_norm_b=g_norm_b,
        w_router=w_router, b_router=b_router,
        w_in=w_in.astype(BF16), w_branch_a=w_branch_a.astype(BF16),
        w_branch_b=w_branch_b.astype(BF16), w_out=w_out.astype(BF16),
        w_gate_dense=w_gate_dense.astype(BF16), w_up_dense=w_up_dense.astype(BF16),
        w_down_dense=w_down_dense.astype(BF16), w_gate_exp=w_gate_exp.astype(BF16),
        w_up_exp=w_up_exp.astype(BF16), w_down_exp=w_down_exp.astype(BF16))
    lb_p = jax.nn.softmax(lb_logits.astype(F32), axis=0)
    lower_bounds = jnp.clip(jnp.cumsum(lb_p, axis=0) - lb_p[:1], 0.0, 1.0)
    c_all = jnp.concatenate([c_prompt, c_sample], axis=0)
    mod = _modulation(c_all, w_mod, b_mod)
    mod = mod.reshape(mod.shape[:3] + (1, D_MODEL))
    pos_prompt = jnp.arange(x_prompt.shape[1], dtype=jnp.int32)
    pos_sample = past_len + jnp.arange(x_sample.shape[1], dtype=jnp.int32)
    t_p, t_s = x_prompt.shape[1], x_sample.shape[1]
    y_p, hg_p, rt_p = _trunk(x_prompt, mod[:, :, :n_prompt], pos_prompt, None, None,
                             lower_bounds, wts, tm=min(512, t_p), tb=min(512, t_p))
    y_s, hg_s, rt_s = _trunk(x_sample, mod[:, :, n_prompt:], pos_sample, state_hgrn, state_ret,
                             lower_bounds, wts, tm=min(512, t_s), tb=min(512, t_s))
    return (y_p, y_s, hg_p, rt_p, hg_s, rt_s)
```

```python
import functools
import math

import jax
import jax.numpy as jnp
from jax import lax
from jax.experimental import pallas as pl
from jax.experimental.pallas import tpu as pltpu

F32 = jnp.float32
BF16 = jnp.bfloat16

D_MODEL = 1024
H_A, DK_A, DV_A = 4, 128, 256
H_B, DK_B, DV_B = 4, 128, 256
ROPE_BASE = 10000.0
FFN_DIM = 2816
N_EXPERTS = 8
RMS_EPS = 1e-6
F_FLOOR = 1e-20
N_MOD = 6
N_IN = 8 * D_MODEL
OFF_QA, OFF_KA, OFF_IA, OFF_OGA = 0, 512, 1024, 2048
OFF_QB, OFF_KB, OFF_VB, OFF_SGB = 3072, 3584, 4096, 5120
OFF_MGA, OFF_MGB = 6144, 7168
REC_COLS = 6144
SUB = 16
LOG2_GAMMA = tuple(math.log2(1.0 - 2.0 ** (-5.0 - h)) for h in range(H_B))
VMEM_LIMIT = 56 * 1024 * 1024
MXU_N = 256


def _params(sem):
    return pltpu.CompilerParams(dimension_semantics=sem, vmem_limit_bytes=VMEM_LIMIT)


def _resident(shape, index_map):
    return pl.BlockSpec(shape, index_map, pipeline_mode=pl.Buffered(1))


def _rms(x, gain):
    return x * lax.rsqrt(jnp.mean(x * x, axis=-1, keepdims=True) + RMS_EPS) * gain


def _sigmoid(x):
    return 1.0 / (1.0 + jnp.exp(-x))


def _dot(a, b):
    return jnp.dot(a, b, preferred_element_type=F32)


def _dot_nt(a, b):
    return lax.dot_general(a, b, (((1,), (1,)), ((), ())), preferred_element_type=F32)


def _dot_tn(a, b):
    return lax.dot_general(a, b, (((0,), (0,)), ((), ())), preferred_element_type=F32)


def _split_bf16(x):
    hi = x.astype(BF16)
    return hi, (x - hi.astype(F32)).astype(BF16)


def _mod_kernel(c_ref, w_ref, b_ref, o_ref):
    c = c_ref[...]
    c_act = (c * _sigmoid(c)).astype(BF16)
    o_ref[...] = _dot(c_act, w_ref[...].astype(BF16)) + b_ref[...]


def _modulation(c_all, w_mod, b_mod):
    depth = w_mod.shape[0]
    n_seq = c_all.shape[0]
    d = D_MODEL
    return pl.pallas_call(
        _mod_kernel,
        grid=(depth, N_MOD),
        in_specs=[
            pl.BlockSpec((n_seq, d), lambda l, j: (0, 0)),
            pl.BlockSpec((None, d, d), lambda l, j: (l, 0, j)),
            pl.BlockSpec((None, None, 1, d), lambda l, j: (l, j, 0, 0)),
        ],
        out_specs=pl.BlockSpec((None, None, n_seq, d), lambda l, j: (l, j, 0, 0)),
        out_shape=jax.ShapeDtypeStruct((depth, N_MOD, n_seq, d), F32),
        compiler_params=_params(("parallel", "parallel")),
        name="modulation",
    )(c_all, w_mod, b_mod.reshape(depth, N_MOD, 1, d))


def _inproj_kernel(x_ref, mod_ref, gain_ref, lb_ref, cos_ref, sin_ref, w_ref,
                   o_ref, ghi_ref, glo_ref):
    d = D_MODEL
    h = _rms(x_ref[...], gain_ref[...]) * (1.0 + mod_ref[1]) + mod_ref[0]
    hb = h.astype(BF16)

    def seg(n):
        return _dot(hb, w_ref[:, n * d:(n + 1) * d])

    y = seg(0)
    q, f_raw = y[:, :512], y[:, 512:]
    o_ref[:, OFF_QA:OFF_QA + 512] = (q * _sigmoid(q)).astype(BF16)
    lb = lb_ref[...]
    sig = _sigmoid(f_raw)
    g2 = jnp.log2(jnp.maximum(lb + (1.0 - lb) * sig, F_FLOOR))
    ghi, glo = _split_bf16(g2)
    ghi_ref[...] = ghi
    glo_ref[...] = glo
    o_ref[:, OFF_KA:OFF_KA + 512] = ((1.0 - lb) * (1.0 - sig)).astype(BF16)
    o_ref[:, OFF_IA:OFF_IA + d] = seg(1).astype(BF16)
    o_ref[:, OFF_OGA:OFF_OGA + d] = _sigmoid(seg(2)).astype(BF16)
    y = seg(3)
    cos, sin = cos_ref[...], sin_ref[...]
    for hh in range(H_B):
        xq = y[:, hh * DK_B:(hh + 1) * DK_B]
        xk = y[:, 512 + hh * DK_B:512 + (hh + 1) * DK_B]
        rq = xq * cos + pltpu.roll(xq, DK_B // 2, axis=1) * sin
        rk = (xk * cos + pltpu.roll(xk, DK_B // 2, axis=1) * sin) * (DK_B ** -0.5)
        o_ref[:, OFF_QB + hh * DK_B:OFF_QB + (hh + 1) * DK_B] = rq.astype(BF16)
        o_ref[:, OFF_KB + hh * DK_B:OFF_KB + (hh + 1) * DK_B] = rk.astype(BF16)
    o_ref[:, OFF_VB:OFF_VB + d] = seg(4).astype(BF16)
    y = seg(5)
    o_ref[:, OFF_SGB:OFF_SGB + d] = (y * _sigmoid(y)).astype(BF16)
    o_ref[:, OFF_MGA:OFF_MGA + d] = _sigmoid(seg(6)).astype(BF16)
    o_ref[:, OFF_MGB:OFF_MGB + d] = _sigmoid(seg(7)).astype(BF16)


def _in_projection(x, mod_l, gain, lb, cos_t, sin_t, w_in_l, tm):
    bsz, t_len, d = x.shape
    g_spec = pl.BlockSpec((None, tm, H_A * DK_A), lambda b, i: (b, i, 0))
    g_shape = jax.ShapeDtypeStruct((bsz, t_len, H_A * DK_A), BF16)
    return pl.pallas_call(
        _inproj_kernel,
        grid=(bsz, t_len // tm),
        in_specs=[
            pl.BlockSpec((None, tm, d), lambda b, i: (b, i, 0)),
            pl.BlockSpec((3, None, 1, d), lambda b, i: (0, b, 0, 0)),
            pl.BlockSpec((1, d), lambda b, i: (0, 0)),
            pl.BlockSpec((1, H_A * DK_A), lambda b, i: (0, 0)),
            pl.BlockSpec((tm, DK_B), lambda b, i: (i, 0)),
            pl.BlockSpec((tm, DK_B), lambda b, i: (i, 0)),
            _resident((d, N_IN), lambda b, i: (0, 0)),
        ],
        out_specs=[pl.BlockSpec((None, tm, N_IN), lambda b, i: (b, i, 0)), g_spec, g_spec],
        out_shape=[jax.ShapeDtypeStruct((bsz, t_len, N_IN), BF16), g_shape, g_shape],
        compiler_params=_params(("parallel", "parallel")),
        name="in_projection",
    )(x, mod_l, gain, lb, cos_t, sin_t, w_in_l)


def _hgrn_diag_lhs(q, k, gc, x_s, row0, l_a):
    half = SUB // 2
    for i in range(l_a // SUB):
        lo = i * SUB
        g_i, q_i = gc[lo:lo + SUB], q[lo:lo + SUB]
        pieces = []
        for s in range(SUB):
            g_s, k_s = g_i[s:s + 1], k[lo + s:lo + s + 1]
            if s < half:
                p = q_i * jnp.exp2(jnp.minimum(g_i - g_s, 0.0)) * k_s
            else:
                p = q_i[half:] * jnp.exp2(jnp.minimum(g_i[half:] - g_s, 0.0)) * k_s
                p = jnp.concatenate([jnp.zeros((half, DK_A), F32), p], axis=0)
            pieces.append(p.astype(BF16))
        x_s[row0 + lo:row0 + lo + SUB, :] = jnp.concatenate(pieces, axis=1)


def _hgrn_off_diag(q, k, gc, l_a):
    q_parts, k_parts = [], []
    for j in range(l_a // SUB - 1):
        lo, hi = j * SUB, (j + 1) * SUB
        g_ref = gc[hi - 1:hi]
        q_hat = q[hi:] * jnp.exp2(gc[hi:] - g_ref)
        k_bar = k[lo:hi] * jnp.exp2(g_ref - gc[lo:hi])
        q_parts.append(jnp.concatenate([jnp.zeros((hi, DK_A), F32), q_hat], axis=0))
        k_rows = [k_bar, jnp.zeros((l_a - hi, DK_A), F32)]
        if lo:
            k_rows.insert(0, jnp.zeros((lo, DK_A), F32))
        k_parts.append(jnp.concatenate(k_rows, axis=0))
    return _dot_nt(jnp.concatenate(q_parts, axis=1).astype(BF16),
                   jnp.concatenate(k_parts, axis=1).astype(BF16))


def _recurrence_kernel(proj_ref, ghi_ref, glo_ref, bsel_ref, gna_ref, gnb_ref, sa0_ref, sb0_ref,
                       oa_ref, ob_ref, sa_out_ref, sb_out_ref, sa_s, sb_s, x_s, *, tb, l_a, l_b):
    t = pl.program_id(1)

    @pl.when(t == 0)
    def _():
        sa_s[...] = sa0_ref[...]
        sb_s[...] = sb0_ref[...]

    ri = lax.broadcasted_iota(jnp.int32, (l_a, l_a), 0)
    ci = lax.broadcasted_iota(jnp.int32, (l_a, l_a), 1)
    tril = (ri >= ci).astype(BF16)
    diag_mask = ((ri // SUB) == (ci // SUB)) & (ri >= ci)

    def chunk_a(c, carry):
        r0 = pl.multiple_of(c * l_a, l_a)
        rows = pl.ds(r0, l_a)
        gc_all = _dot(tril, ghi_ref[rows, :]) + _dot(tril, glo_ref[rows, :])
        q_all = proj_ref[rows, OFF_QA:OFF_QA + H_A * DK_A].astype(F32)
        k_all = proj_ref[rows, OFF_KA:OFF_KA + H_A * DK_A].astype(F32)
        for h in range(H_A):
            cols = slice(h * DK_A, (h + 1) * DK_A)
            _hgrn_diag_lhs(q_all[:, cols], k_all[:, cols], gc_all[:, cols], x_s, h * l_a, l_a)
        a_diag_all = _dot(x_s[...], bsel_ref[...])
        for h in range(H_A):
            cols = slice(h * DK_A, (h + 1) * DK_A)
            q, k, gc = q_all[:, cols], k_all[:, cols], gc_all[:, cols]
            v = proj_ref[rows, OFF_IA + h * DV_A:OFF_IA + (h + 1) * DV_A]
            og = proj_ref[rows, OFF_OGA + h * DV_A:OFF_OGA + (h + 1) * DV_A].astype(F32)
            a = a_diag_all[h * l_a:(h + 1) * l_a]
            if l_a > SUB:
                a = jnp.where(diag_mask, a, _hgrn_off_diag(q, k, gc, l_a))
            else:
                a = jnp.where(diag_mask, a, 0.0)
            s_prev = sa_s[h]
            o = (_dot(a.astype(BF16), v)
                 + _dot((q * jnp.exp2(gc)).astype(BF16), s_prev.astype(BF16)))
            g_end = gc[l_a - 1:l_a]
            k_tail = k * jnp.exp2(g_end - gc)
            packed = jnp.concatenate(
                [k_tail, jnp.broadcast_to(jnp.exp2(g_end), (DK_A - l_a, DK_A))], axis=0)
            packed_t = packed.T
            sa_s[h] = (s_prev * packed_t[:, l_a:l_a + 1]
                       + _dot(packed_t[:, :l_a].astype(BF16), v))
            o = _rms(o, gna_ref[h]) * og
            oa_ref[rows, h * DV_A:(h + 1) * DV_A] = o.astype(BF16)
        return carry

    ti = lax.broadcasted_iota(jnp.int32, (l_b, l_b), 0)
    si = lax.broadcasted_iota(jnp.int32, (l_b, l_b), 1)
    rel = (ti - si).astype(F32)
    tcol = lax.broadcasted_iota(jnp.int32, (l_b, 1), 0).astype(F32)
    decays = [jnp.where(ti >= si, jnp.exp2(jnp.minimum(rel * lg, 0.0)), 0.0) for lg in LOG2_GAMMA]
    q_scale = [jnp.exp2((tcol + 1.0) * lg) for lg in LOG2_GAMMA]
    k_scale = [jnp.exp2((l_b - 1.0 - tcol) * lg) for lg in LOG2_GAMMA]

    def chunk_b(c, carry):
        r0 = pl.multiple_of(c * l_b, l_b)
        rows = pl.ds(r0, l_b)
        for h in range(H_B):
            qb = proj_ref[rows, OFF_QB + h * DK_B:OFF_QB + (h + 1) * DK_B]
            kb = proj_ref[rows, OFF_KB + h * DK_B:OFF_KB + (h + 1) * DK_B]
            v = proj_ref[rows, OFF_VB + h * DV_B:OFF_VB + (h + 1) * DV_B]
            sg = proj_ref[rows, OFF_SGB + h * DV_B:OFF_SGB + (h + 1) * DV_B].astype(F32)
            scores = _dot_nt(qb, kb) * decays[h]
            s_prev = sb_s[h]
            o = (_dot(scores.astype(BF16), v)
                 + _dot((qb.astype(F32) * q_scale[h]).astype(BF16), s_prev.astype(BF16)))
            k_tail = (kb.astype(F32) * k_scale[h]).astype(BF16)
            sb_s[h] = s_prev * (2.0 ** (l_b * LOG2_GAMMA[h])) + _dot_tn(k_tail, v)
            o = _rms(o, gnb_ref[h]) * sg
            ob_ref[rows, h * DV_B:(h + 1) * DV_B] = o.astype(BF16)
        return carry

    def step(c, carry):
        for j in range(l_b // l_a):
            chunk_a(c * (l_b // l_a) + j, carry)
        return chunk_b(c, carry)

    lax.fori_loop(0, tb // l_b, step, 0)

    @pl.when(t == pl.num_programs(1) - 1)
    def _():
        sa_out_ref[...] = sa_s[...]
        sb_out_ref[...] = sb_s[...]


def _diag_selector(l_a):
    s_of_row = jnp.arange(SUB * DK_A, dtype=jnp.int32) // DK_A
    col = jnp.arange(l_a, dtype=jnp.int32)
    return (s_of_row[:, None] == (col[None, :] % SUB)).astype(BF16)


def _recurrences(proj, g_hi, g_lo, gna, gnb, s_a0, s_b0, tb):
    bsz, t_len, _ = proj.shape
    l_a = min(64, t_len)
    l_b = min(128, t_len)
    state_spec = pl.BlockSpec((None, H_A, DK_A, DV_A), lambda b, t: (b, 0, 0, 0))
    act_spec = pl.BlockSpec((None, tb, D_MODEL), lambda b, t: (b, t, 0))
    g_spec = pl.BlockSpec((None, tb, H_A * DK_A), lambda b, t: (b, t, 0))
    return pl.pallas_call(
        functools.partial(_recurrence_kernel, tb=tb, l_a=l_a, l_b=l_b),
        grid=(bsz, t_len // tb),
        in_specs=[
            pl.BlockSpec((None, tb, REC_COLS), lambda b, t: (b, t, 0)),
            g_spec, g_spec,
            pl.BlockSpec((SUB * DK_A, l_a), lambda b, t: (0, 0)),
            pl.BlockSpec((H_A, 1, DV_A), lambda b, t: (0, 0, 0)),
            pl.BlockSpec((H_B, 1, DV_B), lambda b, t: (0, 0, 0)),
            state_spec, state_spec,
        ],
        out_specs=[act_spec, act_spec, state_spec, state_spec],
        out_shape=[
            jax.ShapeDtypeStruct((bsz, t_len, D_MODEL), BF16),
            jax.ShapeDtypeStruct((bsz, t_len, D_MODEL), BF16),
            jax.ShapeDtypeStruct((bsz, H_A, DK_A, DV_A), F32),
            jax.ShapeDtypeStruct((bsz, H_B, DK_B, DV_B), F32),
        ],
        scratch_shapes=[pltpu.VMEM((H_A, DK_A, DV_A), F32), pltpu.VMEM((H_B, DK_B, DV_B), F32),
                        pltpu.VMEM((H_A * l_a, SUB * DK_A), BF16)],
        compiler_params=_params(("parallel", "arbitrary")),
        name="recurrences",
    )(proj, g_hi, g_lo, _diag_selector(l_a), gna, gnb, s_a0, s_b0)


def _merge_kernel(*refs, with_router):
    if with_router:
        (oa_ref, ob_ref, mg_ref, x_ref, mod_ref, gains_ref, wba_ref, wbb_ref, wo_ref,
         wr_ref, br_ref, lt_ref, xo_ref, h2_ref, route_ref, cnt_ref, run_s) = refs
    else:
        (oa_ref, ob_ref, mg_ref, x_ref, mod_ref, gains_ref, wba_ref, wbb_ref, wo_ref,
         xo_ref, h2_ref) = refs
    d = D_MODEL
    y_a = _dot(oa_ref[...], wba_ref[...])
    y_b = _dot(ob_ref[...], wbb_ref[...])
    merged = mg_ref[:, :d].astype(F32) * y_a + mg_ref[:, d:].astype(F32) * y_b
    m_out = _dot(merged.astype(BF16), wo_ref[...])
    x = x_ref[...] + mod_ref[2] * _rms(m_out, gains_ref[1:2])
    xo_ref[...] = x
    h2 = _rms(x, gains_ref[2:3]) * (1.0 + mod_ref[4]) + mod_ref[3]
    if not with_router:
        h2_ref[...] = h2.astype(BF16)
        return
    h2_ref[...] = h2

    @pl.when((pl.program_id(0) == 0) & (pl.program_id(1) == 0))
    def _():
        run_s[...] = jnp.zeros_like(run_s)

    h_hi, h_lo = _split_bf16(h2)
    w_hi, w_lo = _split_bf16(wr_ref[...])
    logits = _dot(h_hi, w_hi) + _dot(h_hi, w_lo) + _dot(h_lo, w_hi) + br_ref[...]
    lane = lax.broadcasted_iota(jnp.int32, logits.shape, 1)
    m1 = jnp.max(logits, axis=-1, keepdims=True)
    i1 = jnp.min(jnp.where(logits == m1, lane, N_EXPERTS), axis=-1, keepdims=True)
    rest = jnp.where(lane == i1, -jnp.inf, logits)
    m2 = jnp.max(rest, axis=-1, keepdims=True)
    i2 = jnp.min(jnp.where(rest == m2, lane, N_EXPERTS), axis=-1, keepdims=True)
    e2 = jnp.exp(m2 - m1)
    w1 = 1.0 / (1.0 + e2)
    w2 = e2 / (1.0 + e2)
    chosen = ((lane == i1) | (lane == i2)).astype(F32)
    before = run_s[...] + _dot(lt_ref[...], chosen.astype(BF16))
    r1 = jnp.sum(jnp.where(lane == i1, before, 0.0), axis=-1, keepdims=True)
    r2 = jnp.sum(jnp.where(lane == i2, before, 0.0), axis=-1, keepdims=True)
    run_s[...] = run_s[...] + jnp.sum(chosen, axis=0, keepdims=True)
    cnt_ref[...] = run_s[...]
    route = jnp.where(lane == 0, i1.astype(F32), i2.astype(F32))
    for j, val in ((2, r1), (3, r2), (4, w1), (5, w2)):
        route = jnp.where(lane == j, val, route)
    route_ref[...] = jnp.where(lane >= 6, 0.0, route)


def _merge(o_a, o_b, proj, x, mod_l, gains_l, wba, wbb, wo, router, tm):
    bsz, t_len, d = x.shape
    with_router = router is not None
    tok = pl.BlockSpec((None, tm, d), lambda b, i: (b, i, 0))
    in_specs = [
        tok, tok,
        pl.BlockSpec((None, tm, 2 * d), lambda b, i: (b, i, OFF_MGA // (2 * d))),
        tok,
        pl.BlockSpec((N_MOD, None, 1, d), lambda b, i: (0, b, 0, 0)),
        pl.BlockSpec((4, d), lambda b, i: (0, 0)),
        _resident((d, d), lambda b, i: (0, 0)),
        _resident((d, d), lambda b, i: (0, 0)),
        _resident((d, d), lambda b, i: (0, 0)),
    ]
    args = [o_a, o_b, proj, x, mod_l, gains_l, wba, wbb, wo]
    if not with_router:
        return pl.pallas_call(
            functools.partial(_merge_kernel, with_router=False),
            grid=(bsz, t_len // tm),
            in_specs=in_specs,
            out_specs=[tok, tok],
            out_shape=[jax.ShapeDtypeStruct((bsz, t_len, d), F32),
                       jax.ShapeDtypeStruct((bsz, t_len, d), BF16)],
            compiler_params=_params(("parallel", "parallel")),
            name="merge",
        )(*args)
    w_r, b_r = router
    strict_lower = jnp.tril(jnp.ones((tm, tm), BF16), -1)
    in_specs += [pl.BlockSpec((d, N_EXPERTS), lambda b, i: (0, 0)),
                 pl.BlockSpec((1, N_EXPERTS), lambda b, i: (0, 0)),
                 pl.BlockSpec((tm, tm), lambda b, i: (0, 0))]
    args += [w_r, b_r.reshape(1, N_EXPERTS), strict_lower]
    return pl.pallas_call(
        functools.partial(_merge_kernel, with_router=True),
        grid=(bsz, t_len // tm),
        in_specs=in_specs,
        out_specs=[tok, tok,
                   pl.BlockSpec((None, tm, N_EXPERTS), lambda b, i: (b, i, 0)),
                   pl.BlockSpec((1, N_EXPERTS), lambda b, i: (0, 0))],
        out_shape=[jax.ShapeDtypeStruct((bsz, t_len, d), F32),
                   jax.ShapeDtypeStruct((bsz, t_len, d), F32),
                   jax.ShapeDtypeStruct((bsz, t_len, N_EXPERTS), F32),
                   jax.ShapeDtypeStruct((1, N_EXPERTS), F32)],
        scratch_shapes=[pltpu.VMEM((1, N_EXPERTS), F32)],
        compiler_params=_params(("arbitrary", "arbitrary")),
        name="merge_router",
    )(*args)


def _swiglu_hidden(hb, wg_ref, wu_ref, a_s):
    for c in range(FFN_DIM // MXU_N):
        sl = slice(c * MXU_N, (c + 1) * MXU_N)
        gate = _dot(hb, wg_ref[:, sl])
        up = _dot(hb, wu_ref[:, sl])
        a_s[:, sl] = (gate * _sigmoid(gate) * up).astype(BF16)


def _ffn_dense_kernel(h_ref, wg_ref, wu_ref, wd_ref, x_ref, mod_ref, gain_ref, o_ref, a_s):
    _swiglu_hidden(h_ref[...], wg_ref, wu_ref, a_s)
    y = _dot(a_s[...], wd_ref[...])
    o_ref[...] = x_ref[...] + mod_ref[0] * _rms(y, gain_ref[...])


def _ffn_dense(h2, wg, wu, wd, x, mod_l, gain, tm):
    bsz, t_len, d = x.shape
    tok = pl.BlockSpec((None, tm, d), lambda b, i: (b, i, 0))
    return pl.pallas_call(
        _ffn_dense_kernel,
        grid=(bsz, t_len // tm),
        in_specs=[
            tok,
            _resident((d, FFN_DIM), lambda b, i: (0, 0)),
            _resident((d, FFN_DIM), lambda b, i: (0, 0)),
            _resident((FFN_DIM, d), lambda b, i: (0, 0)),
            tok,
            pl.BlockSpec((1, None, 1, d), lambda b, i: (N_MOD - 1, b, 0, 0)),
            pl.BlockSpec((1, d), lambda b, i: (0, 0)),
        ],
        out_specs=tok,
        out_shape=jax.ShapeDtypeStruct((bsz, t_len, d), F32),
        scratch_shapes=[pltpu.VMEM((tm, FFN_DIM), BF16)],
        compiler_params=_params(("parallel", "parallel")),
        name="ffn_dense",
    )(h2, wg, wu, wd, x, mod_l, gain)


def _ffn_grouped_kernel(te_ref, nu_ref, x_ref, wg_ref, wu_ref, wd_ref, y_ref, a_s):
    del te_ref

    @pl.when(pl.program_id(0) < nu_ref[0])
    def _():
        _swiglu_hidden(x_ref[...].astype(BF16), wg_ref, wu_ref, a_s)
        y_ref[...] = _dot(a_s[...], wd_ref[...])


def _ffn_grouped(x_sorted, tile_expert, n_used, wg, wu, wd, tmg):
    n_slots, d = x_sorted.shape
    n_tiles = n_slots // tmg

    def row_map(i, te, nu):
        return (jnp.minimum(i, nu[0] - 1), 0)

    def w_map(i, te, nu):
        return (te[i], 0, 0)

    return pl.pallas_call(
        _ffn_grouped_kernel,
        grid_spec=pltpu.PrefetchScalarGridSpec(
            num_scalar_prefetch=2,
            grid=(n_tiles,),
            in_specs=[
                pl.BlockSpec((tmg, d), row_map),
                pl.BlockSpec((None, d, FFN_DIM), w_map),
                pl.BlockSpec((None, d, FFN_DIM), w_map),
                pl.BlockSpec((None, FFN_DIM, d), w_map),
            ],
            out_specs=pl.BlockSpec((tmg, d), row_map),
            scratch_shapes=[pltpu.VMEM((tmg, FFN_DIM), BF16)],
        ),
        out_shape=jax.ShapeDtypeStruct((n_slots, d), F32),
        compiler_params=_params(("arbitrary",)),
        name="ffn_grouped",
    )(tile_expert, n_used, x_sorted, wg, wu, wd)


def _row_copy(src_ref, src_row, dst_ref, dst_row, sem):
    return pltpu.make_async_copy(src_ref.at[pl.ds(src_row, 1)], dst_ref.at[pl.ds(dst_row, 1)], sem)


def _scatter_kernel(s1_ref, s2_ref, h_ref, xs_init_ref, xs_ref, sem):
    del xs_init_ref
    tm = h_ref.shape[0]

    def copies(r):
        return (_row_copy(h_ref, r, xs_ref, s1_ref[0, r], sem),
                _row_copy(h_ref, r, xs_ref, s2_ref[0, r], sem))

    def issue(r, carry):
        for cp in copies(r):
            cp.start()
        return carry

    def drain(r, carry):
        for cp in copies(r):
            cp.wait()
        return carry

    lax.fori_loop(0, tm, issue, 0, unroll=8)
    lax.fori_loop(0, tm, drain, 0, unroll=8)


def _scatter_rows(h2, slot1, slot2, n_slots, tm):
    bsz, t_len, d = h2.shape
    nt = t_len // tm
    slot_spec = pl.BlockSpec((None, 1, tm), lambda b, i: (b * nt + i, 0, 0),
                             memory_space=pltpu.SMEM)
    return pl.pallas_call(
        _scatter_kernel,
        grid=(bsz, nt),
        in_specs=[slot_spec, slot_spec,
                  pl.BlockSpec((tm, d), lambda b, i: (b * nt + i, 0)),
                  pl.BlockSpec(memory_space=pl.ANY)],
        out_specs=pl.BlockSpec(memory_space=pl.ANY),
        out_shape=jax.ShapeDtypeStruct((n_slots, d), F32),
        scratch_shapes=[pltpu.SemaphoreType.DMA],
        input_output_aliases={3: 0},
        compiler_params=_params(("arbitrary", "arbitrary")),
        name="scatter_rows",
    )(slot1, slot2, h2.reshape(bsz * t_len, d), jnp.zeros((n_slots, d), F32))


def _combine_kernel(s1_ref, s2_ref, route_ref, x_ref, mod_ref, gain_ref, y_ref, o_ref,
                    buf1, buf2, sem):
    tm = x_ref.shape[0]

    def copies(r):
        return (_row_copy(y_ref, s1_ref[0, r], buf1, r, sem),
                _row_copy(y_ref, s2_ref[0, r], buf2, r, sem))

    def issue(r, carry):
        for cp in copies(r):
            cp.start()
        return carry

    def drain(r, carry):
        for cp in copies(r):
            cp.wait()
        return carry

    lax.fori_loop(0, tm, issue, 0, unroll=8)
    lax.fori_loop(0, tm, drain, 0, unroll=8)
    f_out = route_ref[:, 4:5] * buf1[...] + route_ref[:, 5:6] * buf2[...]
    o_ref[...] = x_ref[...] + mod_ref[0] * _rms(f_out, gain_ref[...])


def _combine_rows(y_sorted, slot1, slot2, route, x, mod_l, gain, tm):
    bsz, t_len, d = x.shape
    nt = t_len // tm
    slot_spec = pl.BlockSpec((None, 1, tm), lambda b, i: (b * nt + i, 0, 0),
                             memory_space=pltpu.SMEM)
    tok = pl.BlockSpec((None, tm, d), lambda b, i: (b, i, 0))
    return pl.pallas_call(
        _combine_kernel,
        grid=(bsz, nt),
        in_specs=[slot_spec, slot_spec,
                  pl.BlockSpec((None, tm, N_EXPERTS), lambda b, i: (b, i, 0)),
                  tok,
                  pl.BlockSpec((1, None, 1, d), lambda b, i: (N_MOD - 1, b, 0, 0)),
                  pl.BlockSpec((1, d), lambda b, i: (0, 0)),
                  pl.BlockSpec(memory_space=pl.ANY)],
        out_specs=tok,
        out_shape=jax.ShapeDtypeStruct((bsz, t_len, d), F32),
        scratch_shapes=[pltpu.VMEM((tm, d), F32), pltpu.VMEM((tm, d), F32),
                        pltpu.SemaphoreType.DMA],
        compiler_params=_params(("arbitrary", "arbitrary")),
        name="combine_rows",
    )(slot1, slot2, route, x, mod_l, gain, y_sorted)


def _moe_ffn(h2, route, counts, wg, wu, wd, x, mod_l, gain, tm, tmg):
    bsz, t_len, d = x.shape
    n_tok = bsz * t_len
    n_tiles = (2 * n_tok) // tmg + N_EXPERTS
    e1 = route[..., 0].astype(jnp.int32).reshape(n_tok)
    e2 = route[..., 1].astype(jnp.int32).reshape(n_tok)
    r1 = route[..., 2].astype(jnp.int32).reshape(n_tok)
    r2 = route[..., 3].astype(jnp.int32).reshape(n_tok)
    cnt = counts[0].astype(jnp.int32)
    tiles_per = (cnt + tmg - 1) // tmg
    tile_end = jnp.cumsum(tiles_per)
    offs = (tile_end - tiles_per) * tmg
    n_used = tile_end[-1:]
    tile_ids = jnp.arange(n_tiles, dtype=jnp.int32)
    tile_expert = jnp.sum(tile_ids[:, None] >= tile_end[None, :], axis=1).astype(jnp.int32)
    last_expert = jnp.sum(n_used - 1 >= tile_end).astype(jnp.int32)
    tile_expert = jnp.where(tile_ids < n_used, tile_expert, last_expert)
    nt = t_len // tm
    slot1 = (offs[e1] + r1).reshape(bsz * nt, 1, tm)
    slot2 = (offs[e2] + r2).reshape(bsz * nt, 1, tm)
    x_sorted = _scatter_rows(h2, slot1, slot2, n_tiles * tmg, tm)
    y_sorted = _ffn_grouped(x_sorted, tile_expert, n_used, wg, wu, wd, tmg)
    return _combine_rows(y_sorted, slot1, slot2, route, x, mod_l, gain, tm)


def _rope_tables(pos):
    half = DK_B // 2
    inv = ROPE_BASE ** (-jnp.arange(half, dtype=F32) / half)
    ang = pos.astype(F32)[:, None] * inv[None, :]
    cos, sin = jnp.cos(ang), jnp.sin(ang)
    return jnp.concatenate([cos, cos], axis=-1), jnp.concatenate([-sin, sin], axis=-1)


def _trunk(x, mod, pos, s_a_in, s_b_in, lower_bounds, wts, tm, tb, tmg):
    depth = mod.shape[0]
    bsz = x.shape[0]
    cos_t, sin_t = _rope_tables(pos)
    sa_out, sb_out = [], []
    for l in range(depth):
        mod_l = mod[l]
        gains_l = wts["norm_gains"][l]
        if s_a_in is None:
            s_a = jnp.zeros((bsz, H_A, DK_A, DV_A), F32)
            s_b = jnp.zeros((bsz, H_B, DK_B, DV_B), F32)
        else:
            s_a, s_b = s_a_in[l], s_b_in[l]
        proj, g_hi, g_lo = _in_projection(x, mod_l, gains_l[0:1], lower_bounds[l:l + 1],
                                          cos_t, sin_t, wts["w_in"][l], tm)
        o_a, o_b, s_a_new, s_b_new = _recurrences(
            proj, g_hi, g_lo,
            wts["g_norm_a"][l].reshape(H_A, 1, DV_A), wts["g_norm_b"][l].reshape(H_B, 1, DV_B),
            s_a, s_b, tb)
        j = l // 2
        router = None if l % 2 == 0 else (wts["w_router"][j], wts["b_router"][j])
        outs = _merge(o_a, o_b, proj, x, mod_l, gains_l, wts["w_branch_a"][l],
                      wts["w_branch_b"][l], wts["w_out"][l], router, tm)
        if l % 2 == 0:
            x, h2 = outs
            x = _ffn_dense(h2, wts["w_gate_dense"][j], wts["w_up_dense"][j],
                           wts["w_down_dense"][j], x, mod_l, gains_l[3:4], tm)
        else:
            x, h2, route, counts = outs
            x = _moe_ffn(h2, route, counts, wts["w_gate_exp"][j], wts["w_up_exp"][j],
                         wts["w_down_exp"][j], x, mod_l, gains_l[3:4], tm, tmg)
        sa_out.append(s_a_new)
        sb_out.append(s_b_new)
    return x, jnp.stack(sa_out), jnp.stack(sb_out)


def kernel(x_prompt, x_sample, state_hgrn, state_ret, c_prompt, c_sample, w_mod, b_mod, norm_gains,
           w_in, lb_logits, g_norm_a, g_norm_b, w_branch_a, w_branch_b, w_out, w_gate_dense,
           w_up_dense, w_down_dense, w_router, b_router, w_gate_exp, w_up_exp, w_down_exp):
    n_prompt = x_prompt.shape[0]
    past_len = 1024
    wts = dict(
        norm_gains=norm_gains, g_norm_a=g_norm_a, g_norm_b=g_norm_b,
        w_router=w_router, b_router=b_router,
        w_in=w_in.astype(BF16), w_branch_a=w_branch_a.astype(BF16),
        w_branch_b=w_branch_b.astype(BF16), w_out=w_out.astype(BF16),
        w_gate_dense=w_gate_dense.astype(BF16), w_up_dense=w_up_dense.astype(BF16),
        w_down_dense=w_down_dense.astype(BF16), w_gate_exp=w_gate_exp.astype(BF16),
        w_up_exp=w_up_exp.astype(BF16), w_down_exp=w_down_exp.astype(BF16))
    lb_p = jax.nn.softmax(lb_logits.astype(F32), axis=0)
    lower_bounds = jnp.clip(jnp.cumsum(lb_p, axis=0) - lb_p[:1], 0.0, 1.0)
    c_all = jnp.concatenate([c_prompt, c_sample], axis=0)
    mod = _modulation(c_all, w_mod, b_mod)
    mod = mod.reshape(mod.shape[:3] + (1, D_MODEL))
    pos_prompt = jnp.arange(x_prompt.shape[1], dtype=jnp.int32)
    pos_sample = past_len + jnp.arange(x_sample.shape[1], dtype=jnp.int32)
    t_p, t_s = x_prompt.shape[1], x_sample.shape[1]
    y_p, hg_p, rt_p = _trunk(x_prompt, mod[:, :, :n_prompt], pos_prompt, None, None,
                             lower_bounds, wts, tm=min(512, t_p), tb=min(512, t_p), tmg=512)
    y_s, hg_s, rt_s = _trunk(x_sample, mod[:, :, n_prompt:], pos_sample, state_hgrn, state_ret,
                             lower_bounds, wts, tm=min(512, t_s), tb=min(512, t_s), tmg=64)
    return (y_p, y_s, hg_p, rt_p, hg_s, rt_s)
```

```python
import functools
import math

import jax
import jax.numpy as jnp
from jax import lax
from jax.experimental import pallas as pl
from jax.experimental.pallas import tpu as pltpu

F32 = jnp.float32
BF16 = jnp.bfloat16

D_MODEL = 1024
H_A, DK_A, DV_A = 4, 128, 256
H_B, DK_B, DV_B = 4, 128, 256
ROPE_BASE = 10000.0
FFN_DIM = 2816
N_EXPERTS = 8
RMS_EPS = 1e-6
F_FLOOR = 1e-20
N_MOD = 6
N_IN = 8 * D_MODEL
OFF_QA, OFF_KA, OFF_IA, OFF_OGA = 0, 512, 1024, 2048
OFF_QB, OFF_KB, OFF_VB, OFF_SGB = 3072, 3584, 4096, 5120
OFF_MGA, OFF_MGB = 6144, 7168
REC_COLS = 6144
SUB = 16
LOG2_GAMMA = tuple(math.log2(1.0 - 2.0 ** (-5.0 - h)) for h in range(H_B))
VMEM_LIMIT = 56 * 1024 * 1024
MXU_N = 256
ROW_TILE = (8, 128)


def _params(sem):
    return pltpu.CompilerParams(dimension_semantics=sem, vmem_limit_bytes=VMEM_LIMIT)


def _resident(shape, index_map):
    return pl.BlockSpec(shape, index_map, pipeline_mode=pl.Buffered(1))


def _rms(x, gain):
    return x * lax.rsqrt(jnp.mean(x * x, axis=-1, keepdims=True) + RMS_EPS) * gain


def _sigmoid(x):
    return 1.0 / (1.0 + jnp.exp(-x))


def _dot(a, b):
    return jnp.dot(a, b, preferred_element_type=F32)


def _dot_nt(a, b):
    return lax.dot_general(a, b, (((1,), (1,)), ((), ())), preferred_element_type=F32)


def _dot_tn(a, b):
    return lax.dot_general(a, b, (((0,), (0,)), ((), ())), preferred_element_type=F32)


def _split_bf16(x):
    hi = x.astype(BF16)
    return hi, (x - hi.astype(F32)).astype(BF16)


def _mod_kernel(c_ref, w_ref, b_ref, o_ref):
    c = c_ref[...]
    c_act = (c * _sigmoid(c)).astype(BF16)
    o_ref[...] = _dot(c_act, w_ref[...].astype(BF16)) + b_ref[...]


def _modulation(c_all, w_mod, b_mod):
    depth = w_mod.shape[0]
    n_seq = c_all.shape[0]
    d = D_MODEL
    return pl.pallas_call(
        _mod_kernel,
        grid=(depth, N_MOD),
        in_specs=[
            pl.BlockSpec((n_seq, d), lambda l, j: (0, 0)),
            pl.BlockSpec((None, d, d), lambda l, j: (l, 0, j)),
            pl.BlockSpec((None, None, 1, d), lambda l, j: (l, j, 0, 0)),
        ],
        out_specs=pl.BlockSpec((None, None, n_seq, d), lambda l, j: (l, j, 0, 0)),
        out_shape=jax.ShapeDtypeStruct((depth, N_MOD, n_seq, d), F32),
        compiler_params=_params(("parallel", "parallel")),
        name="modulation",
    )(c_all, w_mod, b_mod.reshape(depth, N_MOD, 1, d))


def _inproj_kernel(x_ref, mod_ref, gain_ref, lb_ref, cos_ref, sin_ref, w_ref,
                   o_ref, ghi_ref, glo_ref):
    d = D_MODEL
    h = _rms(x_ref[...], gain_ref[...]) * (1.0 + mod_ref[1]) + mod_ref[0]
    hb = h.astype(BF16)

    def seg(n):
        return _dot(hb, w_ref[:, n * d:(n + 1) * d])

    def hgrn_qf(y):
        q, f_raw = y[:, :512], y[:, 512:]
        o_ref[:, OFF_QA:OFF_QA + 512] = (q * _sigmoid(q)).astype(BF16)
        lb = lb_ref[...]
        sig = _sigmoid(f_raw)
        g2 = jnp.log2(jnp.maximum(lb + (1.0 - lb) * sig, F_FLOOR))
        ghi, glo = _split_bf16(g2)
        ghi_ref[...] = ghi
        glo_ref[...] = glo
        o_ref[:, OFF_KA:OFF_KA + 512] = ((1.0 - lb) * (1.0 - sig)).astype(BF16)

    def retention_qk(y):
        cos, sin = cos_ref[...], sin_ref[...]
        for hh in range(H_B):
            xq = y[:, hh * DK_B:(hh + 1) * DK_B]
            xk = y[:, 512 + hh * DK_B:512 + (hh + 1) * DK_B]
            rq = xq * cos + pltpu.roll(xq, DK_B // 2, axis=1) * sin
            rk = (xk * cos + pltpu.roll(xk, DK_B // 2, axis=1) * sin) * (DK_B ** -0.5)
            o_ref[:, OFF_QB + hh * DK_B:OFF_QB + (hh + 1) * DK_B] = rq.astype(BF16)
            o_ref[:, OFF_KB + hh * DK_B:OFF_KB + (hh + 1) * DK_B] = rk.astype(BF16)

    def store(off, act):
        def epilogue(y):
            o_ref[:, off:off + d] = act(y).astype(BF16)
        return epilogue

    plan = [(1, store(OFF_IA, lambda y: y)), (0, hgrn_qf), (2, store(OFF_OGA, _sigmoid)),
            (3, retention_qk), (5, store(OFF_SGB, lambda y: y * _sigmoid(y))),
            (6, store(OFF_MGA, _sigmoid)), (7, store(OFF_MGB, _sigmoid)),
            (4, store(OFF_VB, lambda y: y))]
    y = seg(plan[0][0])
    for i, (_, epilogue) in enumerate(plan):
        y_next = seg(plan[i + 1][0]) if i + 1 < len(plan) else None
        epilogue(y)
        y = y_next


def _in_projection(x, mod_l, gain, lb, cos_t, sin_t, w_in_l, tm):
    bsz, t_len, d = x.shape
    g_spec = pl.BlockSpec((None, tm, H_A * DK_A), lambda b, i: (b, i, 0))
    g_shape = jax.ShapeDtypeStruct((bsz, t_len, H_A * DK_A), BF16)
    return pl.pallas_call(
        _inproj_kernel,
        grid=(bsz, t_len // tm),
        in_specs=[
            pl.BlockSpec((None, tm, d), lambda b, i: (b, i, 0)),
            pl.BlockSpec((3, None, 1, d), lambda b, i: (0, b, 0, 0)),
            pl.BlockSpec((1, d), lambda b, i: (0, 0)),
            pl.BlockSpec((1, H_A * DK_A), lambda b, i: (0, 0)),
            pl.BlockSpec((tm, DK_B), lambda b, i: (i, 0)),
            pl.BlockSpec((tm, DK_B), lambda b, i: (i, 0)),
            _resident((d, N_IN), lambda b, i: (0, 0)),
        ],
        out_specs=[pl.BlockSpec((None, tm, N_IN), lambda b, i: (b, i, 0)), g_spec, g_spec],
        out_shape=[jax.ShapeDtypeStruct((bsz, t_len, N_IN), BF16), g_shape, g_shape],
        compiler_params=_params(("parallel", "parallel")),
        name="in_projection",
    )(x, mod_l, gain, lb, cos_t, sin_t, w_in_l)


def _hgrn_diag_lhs(q, k, gc, x_s, row0, l_a):
    half = SUB // 2
    for i in range(l_a // SUB):
        lo = i * SUB
        g_i, q_i = gc[lo:lo + SUB], q[lo:lo + SUB]
        pieces = []
        for s in range(SUB):
            g_s, k_s = g_i[s:s + 1], k[lo + s:lo + s + 1]
            if s < half:
                p = q_i * jnp.exp2(jnp.minimum(g_i - g_s, 0.0)) * k_s
            else:
                p = q_i[half:] * jnp.exp2(jnp.minimum(g_i[half:] - g_s, 0.0)) * k_s
                p = jnp.concatenate([jnp.zeros((half, DK_A), F32), p], axis=0)
            pieces.append(p.astype(BF16))
        x_s[row0 + lo:row0 + lo + SUB, :] = jnp.concatenate(pieces, axis=1)


def _hgrn_off_diag(q, k, gc, l_a):
    q_parts, k_parts = [], []
    for j in range(l_a // SUB - 1):
        lo, hi = j * SUB, (j + 1) * SUB
        g_ref = gc[hi - 1:hi]
        q_hat = q[hi:] * jnp.exp2(gc[hi:] - g_ref)
        k_bar = k[lo:hi] * jnp.exp2(g_ref - gc[lo:hi])
        q_parts.append(jnp.concatenate([jnp.zeros((hi, DK_A), F32), q_hat], axis=0))
        k_rows = [k_bar, jnp.zeros((l_a - hi, DK_A), F32)]
        if lo:
            k_rows.insert(0, jnp.zeros((lo, DK_A), F32))
        k_parts.append(jnp.concatenate(k_rows, axis=0))
    return _dot_nt(jnp.concatenate(q_parts, axis=1).astype(BF16),
                   jnp.concatenate(k_parts, axis=1).astype(BF16))


def _recurrence_kernel(proj_ref, ghi_ref, glo_ref, bsel_ref, gna_ref, gnb_ref, sa0_ref, sb0_ref,
                       oa_ref, ob_ref, sa_out_ref, sb_out_ref,
                       sa_s, sb_s, x_s, a_sc, qt_sc, pk_sc, *, tb, l_a, l_b):
    t = pl.program_id(1)

    @pl.when(t == 0)
    def _():
        sa_s[...] = sa0_ref[...]
        sb_s[...] = sb0_ref[...]

    ri = lax.broadcasted_iota(jnp.int32, (l_a, l_a), 0)
    ci = lax.broadcasted_iota(jnp.int32, (l_a, l_a), 1)
    tril = (ri >= ci).astype(BF16)
    diag_mask = ((ri // SUB) == (ci // SUB)) & (ri >= ci)

    def chunk_rows(c):
        return pl.ds(pl.multiple_of(c * l_a, l_a), l_a)

    def stage1_prep(c):
        rows = chunk_rows(c)
        gc_all = _dot(tril, ghi_ref[rows, :]) + _dot(tril, glo_ref[rows, :])
        q_all = proj_ref[rows, OFF_QA:OFF_QA + H_A * DK_A].astype(F32)
        k_all = proj_ref[rows, OFF_KA:OFF_KA + H_A * DK_A].astype(F32)
        return gc_all, q_all, k_all

    def stage1_head(prep, h, slot):
        cols = slice(h * DK_A, (h + 1) * DK_A)
        gc, q, k = (a[:, cols] for a in prep)
        _hgrn_diag_lhs(q, k, gc, x_s, h * l_a, l_a)
        a = _dot(x_s[h * l_a:(h + 1) * l_a, :], bsel_ref[...])
        off = _hgrn_off_diag(q, k, gc, l_a) if l_a > SUB else 0.0
        a_sc[slot, h] = jnp.where(diag_mask, a, off).astype(BF16)
        qt_sc[slot, h] = (q * jnp.exp2(gc)).astype(BF16)
        g_end = gc[l_a - 1:l_a]
        k_tail = k * jnp.exp2(g_end - gc)
        packed = jnp.concatenate(
            [k_tail, jnp.broadcast_to(jnp.exp2(g_end), (DK_A - l_a, DK_A))], axis=0)
        pk_sc[slot, h] = packed.T

    def stage2_head(c, h, slot):
        rows = chunk_rows(c)
        v = proj_ref[rows, OFF_IA + h * DV_A:OFF_IA + (h + 1) * DV_A]
        og = proj_ref[rows, OFF_OGA + h * DV_A:OFF_OGA + (h + 1) * DV_A].astype(F32)
        s_prev = sa_s[h]
        o = _dot(a_sc[slot, h], v) + _dot(qt_sc[slot, h], s_prev.astype(BF16))
        packed_t = pk_sc[slot, h]
        sa_s[h] = (s_prev * packed_t[:, l_a:l_a + 1]
                   + _dot(packed_t[:, :l_a].astype(BF16), v))
        o = _rms(o, gna_ref[h]) * og
        oa_ref[rows, h * DV_A:(h + 1) * DV_A] = o.astype(BF16)

    ti = lax.broadcasted_iota(jnp.int32, (l_b, l_b), 0)
    si = lax.broadcasted_iota(jnp.int32, (l_b, l_b), 1)
    rel = (ti - si).astype(F32)
    tcol = lax.broadcasted_iota(jnp.int32, (l_b, 1), 0).astype(F32)
    decays = [jnp.where(ti >= si, jnp.exp2(jnp.minimum(rel * lg, 0.0)), 0.0) for lg in LOG2_GAMMA]
    q_scale = [jnp.exp2((tcol + 1.0) * lg) for lg in LOG2_GAMMA]
    k_scale = [jnp.exp2((l_b - 1.0 - tcol) * lg) for lg in LOG2_GAMMA]

    def retention_head(c, h):
        rows = pl.ds(pl.multiple_of(c * l_b, l_b), l_b)
        qb = proj_ref[rows, OFF_QB + h * DK_B:OFF_QB + (h + 1) * DK_B]
        kb = proj_ref[rows, OFF_KB + h * DK_B:OFF_KB + (h + 1) * DK_B]
        v = proj_ref[rows, OFF_VB + h * DV_B:OFF_VB + (h + 1) * DV_B]
        sg = proj_ref[rows, OFF_SGB + h * DV_B:OFF_SGB + (h + 1) * DV_B].astype(F32)
        scores = _dot_nt(qb, kb) * decays[h]
        s_prev = sb_s[h]
        o = (_dot(scores.astype(BF16), v)
             + _dot((qb.astype(F32) * q_scale[h]).astype(BF16), s_prev.astype(BF16)))
        k_tail = (kb.astype(F32) * k_scale[h]).astype(BF16)
        sb_s[h] = s_prev * (2.0 ** (l_b * LOG2_GAMMA[h])) + _dot_tn(k_tail, v)
        o = _rms(o, gnb_ref[h]) * sg
        ob_ref[rows, h * DV_B:(h + 1) * DV_B] = o.astype(BF16)

    n_a = tb // l_a
    prep = stage1_prep(0)
    for h in range(H_A):
        stage1_head(prep, h, 0)
    if n_a == 1:
        for h in range(H_A):
            stage2_head(0, h, 0)
            retention_head(0, h)
    else:
        def step(s, carry):
            prep = stage1_prep(2 * s + 1)
            for h in range(H_A):
                stage1_head(prep, h, 1)
                stage2_head(2 * s, h, 0)
            prep = stage1_prep(jnp.minimum(2 * s + 2, n_a - 1))
            for h in range(H_A):
                stage1_head(prep, h, 0)
                stage2_head(2 * s + 1, h, 1)
                retention_head(s, h)
            return carry

        lax.fori_loop(0, n_a // 2, step, 0)

    @pl.when(t == pl.num_programs(1) - 1)
    def _():
        sa_out_ref[...] = sa_s[...]
        sb_out_ref[...] = sb_s[...]


def _diag_selector(l_a):
    s_of_row = jnp.arange(SUB * DK_A, dtype=jnp.int32) // DK_A
    col = jnp.arange(l_a, dtype=jnp.int32)
    return (s_of_row[:, None] == (col[None, :] % SUB)).astype(BF16)


def _recurrences(proj, g_hi, g_lo, gna, gnb, s_a0, s_b0, tb):
    bsz, t_len, _ = proj.shape
    l_a = min(64, t_len)
    l_b = min(128, t_len)
    state_spec = pl.BlockSpec((None, H_A, DK_A, DV_A), lambda b, t: (b, 0, 0, 0))
    act_spec = pl.BlockSpec((None, tb, D_MODEL), lambda b, t: (b, t, 0))
    g_spec = pl.BlockSpec((None, tb, H_A * DK_A), lambda b, t: (b, t, 0))
    return pl.pallas_call(
        functools.partial(_recurrence_kernel, tb=tb, l_a=l_a, l_b=l_b),
        grid=(bsz, t_len // tb),
        in_specs=[
            pl.BlockSpec((None, tb, REC_COLS), lambda b, t: (b, t, 0)),
            g_spec, g_spec,
            pl.BlockSpec((SUB * DK_A, l_a), lambda b, t: (0, 0)),
            pl.BlockSpec((H_A, 1, DV_A), lambda b, t: (0, 0, 0)),
            pl.BlockSpec((H_B, 1, DV_B), lambda b, t: (0, 0, 0)),
            state_spec, state_spec,
        ],
        out_specs=[act_spec, act_spec, state_spec, state_spec],
        out_shape=[
            jax.ShapeDtypeStruct((bsz, t_len, D_MODEL), BF16),
            jax.ShapeDtypeStruct((bsz, t_len, D_MODEL), BF16),
            jax.ShapeDtypeStruct((bsz, H_A, DK_A, DV_A), F32),
            jax.ShapeDtypeStruct((bsz, H_B, DK_B, DV_B), F32),
        ],
        scratch_shapes=[pltpu.VMEM((H_A, DK_A, DV_A), F32), pltpu.VMEM((H_B, DK_B, DV_B), F32),
                        pltpu.VMEM((H_A * l_a, SUB * DK_A), BF16),
                        pltpu.VMEM((2, H_A, l_a, l_a), BF16),
                        pltpu.VMEM((2, H_A, l_a, DK_A), BF16),
                        pltpu.VMEM((2, H_A, DK_A, DK_A), F32)],
        compiler_params=_params(("parallel", "arbitrary")),
        name="recurrences",
    )(proj, g_hi, g_lo, _diag_selector(l_a), gna, gnb, s_a0, s_b0)


def _merge_kernel(*refs, with_router):
    if with_router:
        (oa_ref, ob_ref, mg_ref, x_ref, mod_ref, gains_ref, wba_ref, wbb_ref, wo_ref,
         wr_ref, br_ref, lt_ref, xo_ref, h2_ref, route_ref, cnt_ref, run_s) = refs
    else:
        (oa_ref, ob_ref, mg_ref, x_ref, mod_ref, gains_ref, wba_ref, wbb_ref, wo_ref,
         xo_ref, h2_ref) = refs
    d = D_MODEL
    y_a = _dot(oa_ref[...], wba_ref[...])
    y_b = _dot(ob_ref[...], wbb_ref[...])
    merged = mg_ref[:, :d].astype(F32) * y_a + mg_ref[:, d:].astype(F32) * y_b
    m_out = _dot(merged.astype(BF16), wo_ref[...])
    x = x_ref[...] + mod_ref[2] * _rms(m_out, gains_ref[1:2])
    xo_ref[...] = x
    h2 = _rms(x, gains_ref[2:3]) * (1.0 + mod_ref[4]) + mod_ref[3]
    if not with_router:
        h2_ref[...] = h2.astype(BF16)
        return
    h2_ref[...] = h2

    @pl.when((pl.program_id(0) == 0) & (pl.program_id(1) == 0))
    def _():
        run_s[...] = jnp.zeros_like(run_s)

    h_hi, h_lo = _split_bf16(h2)
    w_hi, w_lo = _split_bf16(wr_ref[...])
    logits = _dot(h_hi, w_hi) + _dot(h_hi, w_lo) + _dot(h_lo, w_hi) + br_ref[...]
    lane = lax.broadcasted_iota(jnp.int32, logits.shape, 1)
    m1 = jnp.max(logits, axis=-1, keepdims=True)
    i1 = jnp.min(jnp.where(logits == m1, lane, N_EXPERTS), axis=-1, keepdims=True)
    rest = jnp.where(lane == i1, -jnp.inf, logits)
    m2 = jnp.max(rest, axis=-1, keepdims=True)
    i2 = jnp.min(jnp.where(rest == m2, lane, N_EXPERTS), axis=-1, keepdims=True)
    e2 = jnp.exp(m2 - m1)
    w1 = 1.0 / (1.0 + e2)
    w2 = e2 / (1.0 + e2)
    chosen = ((lane == i1) | (lane == i2)).astype(F32)
    before = run_s[...] + _dot(lt_ref[...], chosen.astype(BF16))
    r1 = jnp.sum(jnp.where(lane == i1, before, 0.0), axis=-1, keepdims=True)
    r2 = jnp.sum(jnp.where(lane == i2, before, 0.0), axis=-1, keepdims=True)
    run_s[...] = run_s[...] + jnp.sum(chosen, axis=0, keepdims=True)
    cnt_ref[...] = run_s[...]
    route = jnp.where(lane == 0, i1.astype(F32), i2.astype(F32))
    for j, val in ((2, r1), (3, r2), (4, w1), (5, w2)):
        route = jnp.where(lane == j, val, route)
    route_ref[...] = jnp.where(lane >= 6, 0.0, route)


def _merge(o_a, o_b, proj, x, mod_l, gains_l, wba, wbb, wo, router, tm):
    bsz, t_len, d = x.shape
    with_router = router is not None
    tok = pl.BlockSpec((None, tm, d), lambda b, i: (b, i, 0))
    in_specs = [
        tok, tok,
        pl.BlockSpec((None, tm, 2 * d), lambda b, i: (b, i, OFF_MGA // (2 * d))),
        tok,
        pl.BlockSpec((N_MOD, None, 1, d), lambda b, i: (0, b, 0, 0)),
        pl.BlockSpec((4, d), lambda b, i: (0, 0)),
        _resident((d, d), lambda b, i: (0, 0)),
        _resident((d, d), lambda b, i: (0, 0)),
        _resident((d, d), lambda b, i: (0, 0)),
    ]
    args = [o_a, o_b, proj, x, mod_l, gains_l, wba, wbb, wo]
    if not with_router:
        return pl.pallas_call(
            functools.partial(_merge_kernel, with_router=False),
            grid=(bsz, t_len // tm),
            in_specs=in_specs,
            out_specs=[tok, tok],
            out_shape=[jax.ShapeDtypeStruct((bsz, t_len, d), F32),
                       jax.ShapeDtypeStruct((bsz, t_len, d), BF16)],
            compiler_params=_params(("parallel", "parallel")),
            name="merge",
        )(*args)
    w_r, b_r = router
    strict_lower = jnp.tril(jnp.ones((tm, tm), BF16), -1)
    in_specs += [pl.BlockSpec((d, N_EXPERTS), lambda b, i: (0, 0)),
                 pl.BlockSpec((1, N_EXPERTS), lambda b, i: (0, 0)),
                 pl.BlockSpec((tm, tm), lambda b, i: (0, 0))]
    args += [w_r, b_r.reshape(1, N_EXPERTS), strict_lower]
    return pl.pallas_call(
        functools.partial(_merge_kernel, with_router=True),
        grid=(bsz, t_len // tm),
        in_specs=in_specs,
        out_specs=[tok, tok,
                   pl.BlockSpec((None, tm, N_EXPERTS), lambda b, i: (b, i, 0)),
                   pl.BlockSpec((1, N_EXPERTS), lambda b, i: (0, 0))],
        out_shape=[jax.ShapeDtypeStruct((bsz, t_len, d), F32),
                   jax.ShapeDtypeStruct((bsz, t_len, d), F32),
                   jax.ShapeDtypeStruct((bsz, t_len, N_EXPERTS), F32),
                   jax.ShapeDtypeStruct((1, N_EXPERTS), F32)],
        scratch_shapes=[pltpu.VMEM((1, N_EXPERTS), F32)],
        compiler_params=_params(("arbitrary", "arbitrary")),
        name="merge_router",
    )(*args)


def _swiglu_hidden(hb, wg_ref, wu_ref, a_s):
    for c in range(FFN_DIM // MXU_N):
        sl = slice(c * MXU_N, (c + 1) * MXU_N)
        gate = _dot(hb, wg_ref[:, sl])
        up = _dot(hb, wu_ref[:, sl])
        a_s[:, sl] = (gate * _sigmoid(gate) * up).astype(BF16)


def _ffn_dense_kernel(h_ref, wg_ref, wu_ref, wd_ref, x_ref, mod_ref, gain_ref, o_ref, a_s):
    _swiglu_hidden(h_ref[...], wg_ref, wu_ref, a_s)
    y = _dot(a_s[...], wd_ref[...])
    o_ref[...] = x_ref[...] + mod_ref[0] * _rms(y, gain_ref[...])


def _ffn_dense(h2, wg, wu, wd, x, mod_l, gain, tm):
    bsz, t_len, d = x.shape
    tok = pl.BlockSpec((None, tm, d), lambda b, i: (b, i, 0))
    return pl.pallas_call(
        _ffn_dense_kernel,
        grid=(bsz, t_len // tm),
        in_specs=[
            tok,
            _resident((d, FFN_DIM), lambda b, i: (0, 0)),
            _resident((d, FFN_DIM), lambda b, i: (0, 0)),
            _resident((FFN_DIM, d), lambda b, i: (0, 0)),
            tok,
            pl.BlockSpec((1, None, 1, d), lambda b, i: (N_MOD - 1, b, 0, 0)),
            pl.BlockSpec((1, d), lambda b, i: (0, 0)),
        ],
        out_specs=tok,
        out_shape=jax.ShapeDtypeStruct((bsz, t_len, d), F32),
        scratch_shapes=[pltpu.VMEM((tm, FFN_DIM), BF16)],
        compiler_params=_params(("parallel", "parallel")),
        name="ffn_dense",
    )(h2, wg, wu, wd, x, mod_l, gain)


def _ffn_grouped_kernel(te_ref, nu_ref, x_ref, wg_ref, wu_ref, wd_ref, y_ref, a_s):
    del te_ref

    @pl.when(pl.program_id(0) < nu_ref[0])
    def _():
        tmg = x_ref.shape[0]
        x = x_ref[...].reshape(tmg, D_MODEL)
        _swiglu_hidden(x.astype(BF16), wg_ref, wu_ref, a_s)
        y_ref[...] = _dot(a_s[...], wd_ref[...]).reshape(y_ref.shape)


def _ffn_grouped(x_sorted, tile_expert, n_used, wg, wu, wd, tmg):
    n_slots = x_sorted.shape[0]
    d = D_MODEL
    n_tiles = n_slots // tmg

    def row_map(i, te, nu):
        return (jnp.minimum(i, nu[0] - 1), 0, 0)

    def w_map(i, te, nu):
        return (te[i], 0, 0)

    return pl.pallas_call(
        _ffn_grouped_kernel,
        grid_spec=pltpu.PrefetchScalarGridSpec(
            num_scalar_prefetch=2,
            grid=(n_tiles,),
            in_specs=[
                pl.BlockSpec((tmg,) + ROW_TILE, row_map),
                pl.BlockSpec((None, d, FFN_DIM), w_map),
                pl.BlockSpec((None, d, FFN_DIM), w_map),
                pl.BlockSpec((None, FFN_DIM, d), w_map),
            ],
            out_specs=pl.BlockSpec((tmg,) + ROW_TILE, row_map),
            scratch_shapes=[pltpu.VMEM((tmg, FFN_DIM), BF16)],
        ),
        out_shape=jax.ShapeDtypeStruct((n_slots,) + ROW_TILE, F32),
        compiler_params=_params(("arbitrary",)),
        name="ffn_grouped",
    )(tile_expert, n_used, x_sorted, wg, wu, wd)


def _row_copy(src_ref, src_row, dst_ref, dst_row, sem):
    return pltpu.make_async_copy(src_ref.at[pl.ds(src_row, 1)], dst_ref.at[pl.ds(dst_row, 1)], sem)


def _run_row_copies(copies, n_rows):
    def issue(r, carry):
        for cp in copies(r):
            cp.start()
        return carry

    def drain(r, carry):
        for cp in copies(r):
            cp.wait()
        return carry

    lax.fori_loop(0, n_rows, issue, 0, unroll=8)
    lax.fori_loop(0, n_rows, drain, 0, unroll=8)


def _scatter_kernel(s1_ref, s2_ref, ztile_ref, h_ref, xs_ref, rows_s, zero_s, sem, zsem):
    tm = h_ref.shape[0]
    tmg = zero_s.shape[0]

    @pl.when((pl.program_id(0) == 0) & (pl.program_id(1) == 0))
    def _():
        zero_s[...] = jnp.zeros_like(zero_s)

        def zero_copy(e):
            return pltpu.make_async_copy(zero_s, xs_ref.at[pl.ds(ztile_ref[e], tmg)], zsem)

        for e in range(N_EXPERTS):
            pl.when(ztile_ref[e] >= 0)(lambda e=e: zero_copy(e).start())
        for e in range(N_EXPERTS):
            pl.when(ztile_ref[e] >= 0)(lambda e=e: zero_copy(e).wait())

    rows_s[...] = h_ref[...].reshape(rows_s.shape)
    _run_row_copies(lambda r: (_row_copy(rows_s, r, xs_ref, s1_ref[0, r], sem),
                               _row_copy(rows_s, r, xs_ref, s2_ref[0, r], sem)), tm)


def _scatter_rows(h2, slot1, slot2, zero_tile_row, n_slots, tm, tmg):
    bsz, t_len, d = h2.shape
    nt = t_len // tm
    slot_spec = pl.BlockSpec((None, 1, tm), lambda b, i: (b * nt + i, 0, 0),
                             memory_space=pltpu.SMEM)
    return pl.pallas_call(
        _scatter_kernel,
        grid=(bsz, nt),
        in_specs=[slot_spec, slot_spec,
                  pl.BlockSpec(memory_space=pltpu.SMEM),
                  pl.BlockSpec((tm, d), lambda b, i: (b * nt + i, 0))],
        out_specs=pl.BlockSpec(memory_space=pl.ANY),
        out_shape=jax.ShapeDtypeStruct((n_slots,) + ROW_TILE, F32),
        scratch_shapes=[pltpu.VMEM((tm,) + ROW_TILE, F32), pltpu.VMEM((tmg,) + ROW_TILE, F32),
                        pltpu.SemaphoreType.DMA, pltpu.SemaphoreType.DMA],
        compiler_params=_params(("arbitrary", "arbitrary")),
        name="scatter_rows",
    )(slot1, slot2, zero_tile_row, h2.reshape(bsz * t_len, d))


def _combine_kernel(s1_ref, s2_ref, route_ref, x_ref, mod_ref, gain_ref, y_ref, o_ref,
                    buf1, buf2, sem):
    tm, d = x_ref.shape
    _run_row_copies(lambda r: (_row_copy(y_ref, s1_ref[0, r], buf1, r, sem),
                               _row_copy(y_ref, s2_ref[0, r], buf2, r, sem)), tm)
    f_out = (route_ref[:, 4:5] * buf1[...].reshape(tm, d)
             + route_ref[:, 5:6] * buf2[...].reshape(tm, d))
    o_ref[...] = x_ref[...] + mod_ref[0] * _rms(f_out, gain_ref[...])


def _combine_rows(y_sorted, slot1, slot2, route, x, mod_l, gain, tm):
    bsz, t_len, d = x.shape
    nt = t_len // tm
    slot_spec = pl.BlockSpec((None, 1, tm), lambda b, i: (b * nt + i, 0, 0),
                             memory_space=pltpu.SMEM)
    tok = pl.BlockSpec((None, tm, d), lambda b, i: (b, i, 0))
    return pl.pallas_call(
        _combine_kernel,
        grid=(bsz, nt),
        in_specs=[slot_spec, slot_spec,
                  pl.BlockSpec((None, tm, N_EXPERTS), lambda b, i: (b, i, 0)),
                  tok,
                  pl.BlockSpec((1, None, 1, d), lambda b, i: (N_MOD - 1, b, 0, 0)),
                  pl.BlockSpec((1, d), lambda b, i: (0, 0)),
                  pl.BlockSpec(memory_space=pl.ANY)],
        out_specs=tok,
        out_shape=jax.ShapeDtypeStruct((bsz, t_len, d), F32),
        scratch_shapes=[pltpu.VMEM((tm,) + ROW_TILE, F32), pltpu.VMEM((tm,) + ROW_TILE, F32),
                        pltpu.SemaphoreType.DMA],
        compiler_params=_params(("arbitrary", "arbitrary")),
        name="combine_rows",
    )(slot1, slot2, route, x, mod_l, gain, y_sorted)


def _moe_ffn(h2, route, counts, wg, wu, wd, x, mod_l, gain, tm, tmg):
    bsz, t_len, d = x.shape
    n_tok = bsz * t_len
    n_tiles = (2 * n_tok) // tmg + N_EXPERTS
    e1 = route[..., 0].astype(jnp.int32).reshape(n_tok)
    e2 = route[..., 1].astype(jnp.int32).reshape(n_tok)
    r1 = route[..., 2].astype(jnp.int32).reshape(n_tok)
    r2 = route[..., 3].astype(jnp.int32).reshape(n_tok)
    cnt = counts[0].astype(jnp.int32)
    tiles_per = (cnt + tmg - 1) // tmg
    tile_end = jnp.cumsum(tiles_per)
    offs = (tile_end - tiles_per) * tmg
    n_used = tile_end[-1:]
    tile_ids = jnp.arange(n_tiles, dtype=jnp.int32)
    tile_expert = jnp.sum(tile_ids[:, None] >= tile_end[None, :], axis=1).astype(jnp.int32)
    last_expert = jnp.sum(n_used - 1 >= tile_end).astype(jnp.int32)
    tile_expert = jnp.where(tile_ids < n_used, tile_expert, last_expert)
    zero_tile_row = jnp.where(cnt > 0, (tile_end - 1) * tmg, -1).astype(jnp.int32)
    nt = t_len // tm
    slot1 = (offs[e1] + r1).reshape(bsz * nt, 1, tm)
    slot2 = (offs[e2] + r2).reshape(bsz * nt, 1, tm)
    x_sorted = _scatter_rows(h2, slot1, slot2, zero_tile_row, n_tiles * tmg, tm, tmg)
    y_sorted = _ffn_grouped(x_sorted, tile_expert, n_used, wg, wu, wd, tmg)
    return _combine_rows(y_sorted, slot1, slot2, route, x, mod_l, gain, tm)


def _rope_tables(pos):
    half = DK_B // 2
    inv = ROPE_BASE ** (-jnp.arange(half, dtype=F32) / half)
    ang = pos.astype(F32)[:, None] * inv[None, :]
    cos, sin = jnp.cos(ang), jnp.sin(ang)
    return jnp.concatenate([cos, cos], axis=-1), jnp.concatenate([-sin, sin], axis=-1)


def _trunk(x, mod, pos, s_a_in, s_b_in, lower_bounds, wts, tm, tb, tmg):
    depth = mod.shape[0]
    bsz = x.shape[0]
    cos_t, sin_t = _rope_tables(pos)
    sa_out, sb_out = [], []
    for l in range(depth):
        mod_l = mod[l]
        gains_l = wts["norm_gains"][l]
        if s_a_in is None:
            s_a = jnp.zeros((bsz, H_A, DK_A, DV_A), F32)
            s_b = jnp.zeros((bsz, H_B, DK_B, DV_B), F32)
        else:
            s_a, s_b = s_a_in[l], s_b_in[l]
        proj, g_hi, g_lo = _in_projection(x, mod_l, gains_l[0:1], lower_bounds[l:l + 1],
                                          cos_t, sin_t, wts["w_in"][l], tm)
        o_a, o_b, s_a_new, s_b_new = _recurrences(
            proj, g_hi, g_lo,
            wts["g_norm_a"][l].reshape(H_A, 1, DV_A), wts["g_norm_b"][l].reshape(H_B, 1, DV_B),
            s_a, s_b, tb)
        j = l // 2
        router = None if l % 2 == 0 else (wts["w_router"][j], wts["b_router"][j])
        outs = _merge(o_a, o_b, proj, x, mod_l, gains_l, wts["w_branch_a"][l],
                      wts["w_branch_b"][l], wts["w_out"][l], router, tm)
        if l % 2 == 0:
            x, h2 = outs
            x = _ffn_dense(h2, wts["w_gate_dense"][j], wts["w_up_dense"][j],
                           wts["w_down_dense"][j], x, mod_l, gains_l[3:4], tm)
        else:
            x, h2, route, counts = outs
            x = _moe_ffn(h2, route, counts, wts["w_gate_exp"][j], wts["w_up_exp"][j],
                         wts["w_down_exp"][j], x, mod_l, gains_l[3:4], tm, tmg)
        sa_out.append(s_a_new)
        sb_out.append(s_b_new)
    return x, jnp.stack(sa_out), jnp.stack(sb_out)


def kernel(x_prompt, x_sample, state_hgrn, state_ret, c_prompt, c_sample, w_mod, b_mod, norm_gains,
           w_in, lb_logits, g_norm_a, g_norm_b, w_branch_a, w_branch_b, w_out, w_gate_dense,
           w_up_dense, w_down_dense, w_router, b_router, w_gate_exp, w_up_exp, w_down_exp):
    n_prompt = x_prompt.shape[0]
    past_len = 1024
    wts = dict(
        norm_gains=norm_gains, g_norm_a=g_norm_a, g_norm_b=g_norm_b,
        w_router=w_router, b_router=b_router,
        w_in=w_in.astype(BF16), w_branch_a=w_branch_a.astype(BF16),
        w_branch_b=w_branch_b.astype(BF16), w_out=w_out.astype(BF16),
        w_gate_dense=w_gate_dense.astype(BF16), w_up_dense=w_up_dense.astype(BF16),
        w_down_dense=w_down_dense.astype(BF16), w_gate_exp=w_gate_exp.astype(BF16),
        w_up_exp=w_up_exp.astype(BF16), w_down_exp=w_down_exp.astype(BF16))
    lb_p = jax.nn.softmax(lb_logits.astype(F32), axis=0)
    lower_bounds = jnp.clip(jnp.cumsum(lb_p, axis=0) - lb_p[:1], 0.0, 1.0)
    c_all = jnp.concatenate([c_prompt, c_sample], axis=0)
    mod = _modulation(c_all, w_mod, b_mod)
    mod = mod.reshape(mod.shape[:3] + (1, D_MODEL))
    pos_prompt = jnp.arange(x_prompt.shape[1], dtype=jnp.int32)
    pos_sample = past_len + jnp.arange(x_sample.shape[1], dtype=jnp.int32)
    t_p, t_s = x_prompt.shape[1], x_sample.shape[1]
    y_p, hg_p, rt_p = _trunk(x_prompt, mod[:, :, :n_prompt], pos_prompt, None, None,
                             lower_bounds, wts, tm=min(512, t_p), tb=min(1024, t_p), tmg=512)
    y_s, hg_s, rt_s = _trunk(x_sample, mod[:, :, n_prompt:], pos_sample, state_hgrn, state_ret,
                             lower_bounds, wts, tm=min(512, t_s), tb=min(512, t_s), tmg=64)
    return (y_p, y_s, hg_p, rt_p, hg_s, rt_s)
```

```python
import functools
import math

import jax
import jax.numpy as jnp
from jax import lax
from jax.experimental import pallas as pl
from jax.experimental.pallas import tpu as pltpu

F32 = jnp.float32
BF16 = jnp.bfloat16

D_MODEL = 1024
H_A, DK_A, DV_A = 4, 128, 256
H_B, DK_B, DV_B = 4, 128, 256
ROPE_BASE = 10000.0
FFN_DIM = 2816
N_EXPERTS = 8
RMS_EPS = 1e-6
F_FLOOR = 1e-20
N_MOD = 6
N_IN = 8 * D_MODEL
OFF_QA, OFF_KA, OFF_IA, OFF_OGA = 0, 512, 1024, 2048
OFF_QB, OFF_KB, OFF_VB, OFF_SGB = 3072, 3584, 4096, 5120
OFF_MGA, OFF_MGB = 6144, 7168
REC_COLS = 6144
SUB = 16
LOG2_GAMMA = tuple(math.log2(1.0 - 2.0 ** (-5.0 - h)) for h in range(H_B))
VMEM_LIMIT = 56 * 1024 * 1024
MXU_N = 256
ROW_TILE = (8, 128)


def _params(sem):
    return pltpu.CompilerParams(dimension_semantics=sem, vmem_limit_bytes=VMEM_LIMIT)


def _resident(shape, index_map):
    return pl.BlockSpec(shape, index_map, pipeline_mode=pl.Buffered(1))


def _rms(x, gain):
    return x * lax.rsqrt(jnp.mean(x * x, axis=-1, keepdims=True) + RMS_EPS) * gain


def _sigmoid(x):
    return 1.0 / (1.0 + jnp.exp(-x))


def _dot(a, b):
    return jnp.dot(a, b, preferred_element_type=F32)


def _dot_nt(a, b):
    return lax.dot_general(a, b, (((1,), (1,)), ((), ())), preferred_element_type=F32)


def _dot_tn(a, b):
    return lax.dot_general(a, b, (((0,), (0,)), ((), ())), preferred_element_type=F32)


def _split_bf16(x):
    hi = x.astype(BF16)
    return hi, (x - hi.astype(F32)).astype(BF16)


def _mod_kernel(c_ref, w_ref, b_ref, o_ref):
    c = c_ref[...]
    c_act = (c * _sigmoid(c)).astype(BF16)
    o_ref[...] = _dot(c_act, w_ref[...].astype(BF16)) + b_ref[...]


def _modulation(c_all, w_mod, b_mod):
    depth = w_mod.shape[0]
    n_seq = c_all.shape[0]
    d = D_MODEL
    return pl.pallas_call(
        _mod_kernel,
        grid=(depth, N_MOD),
        in_specs=[
            pl.BlockSpec((n_seq, d), lambda l, j: (0, 0)),
            pl.BlockSpec((None, d, d), lambda l, j: (l, 0, j)),
            pl.BlockSpec((None, None, 1, d), lambda l, j: (l, j, 0, 0)),
        ],
        out_specs=pl.BlockSpec((None, None, n_seq, d), lambda l, j: (l, j, 0, 0)),
        out_shape=jax.ShapeDtypeStruct((depth, N_MOD, n_seq, d), F32),
        compiler_params=_params(("parallel", "parallel")),
        name="modulation",
    )(c_all, w_mod, b_mod.reshape(depth, N_MOD, 1, d))


def _inproj_kernel(x_ref, mod_ref, gain_ref, lb_ref, cos_ref, sin_ref, w_ref,
                   o_ref, ghi_ref, glo_ref):
    d = D_MODEL
    h = _rms(x_ref[...], gain_ref[...]) * (1.0 + mod_ref[1]) + mod_ref[0]
    hb = h.astype(BF16)

    def seg(n):
        return _dot(hb, w_ref[:, n * d:(n + 1) * d])

    def hgrn_qf(y):
        q, f_raw = y[:, :512], y[:, 512:]
        o_ref[:, OFF_QA:OFF_QA + 512] = (q * _sigmoid(q)).astype(BF16)
        lb = lb_ref[...]
        sig = _sigmoid(f_raw)
        g2 = jnp.log2(jnp.maximum(lb + (1.0 - lb) * sig, F_FLOOR))
        ghi, glo = _split_bf16(g2)
        ghi_ref[...] = ghi
        glo_ref[...] = glo
        o_ref[:, OFF_KA:OFF_KA + 512] = ((1.0 - lb) * (1.0 - sig)).astype(BF16)

    def retention_qk(y):
        cos, sin = cos_ref[...], sin_ref[...]
        for hh in range(H_B):
            xq = y[:, hh * DK_B:(hh + 1) * DK_B]
            xk = y[:, 512 + hh * DK_B:512 + (hh + 1) * DK_B]
            rq = xq * cos + pltpu.roll(xq, DK_B // 2, axis=1) * sin
            rk = (xk * cos + pltpu.roll(xk, DK_B // 2, axis=1) * sin) * (DK_B ** -0.5)
            o_ref[:, OFF_QB + hh * DK_B:OFF_QB + (hh + 1) * DK_B] = rq.astype(BF16)
            o_ref[:, OFF_KB + hh * DK_B:OFF_KB + (hh + 1) * DK_B] = rk.astype(BF16)

    def store(off, act):
        def epilogue(y):
            o_ref[:, off:off + d] = act(y).astype(BF16)
        return epilogue

    plan = [(1, store(OFF_IA, lambda y: y)), (0, hgrn_qf), (2, store(OFF_OGA, _sigmoid)),
            (3, retention_qk), (5, store(OFF_SGB, lambda y: y * _sigmoid(y))),
            (6, store(OFF_MGA, _sigmoid)), (7, store(OFF_MGB, _sigmoid)),
            (4, store(OFF_VB, lambda y: y))]
    y = seg(plan[0][0])
    for i, (_, epilogue) in enumerate(plan):
        y_next = seg(plan[i + 1][0]) if i + 1 < len(plan) else None
        epilogue(y)
        y = y_next


def _in_projection(x, mod_l, gain, lb, cos_t, sin_t, w_in_l, tm):
    bsz, t_len, d = x.shape
    g_spec = pl.BlockSpec((None, tm, H_A * DK_A), lambda b, i: (b, i, 0))
    g_shape = jax.ShapeDtypeStruct((bsz, t_len, H_A * DK_A), BF16)
    return pl.pallas_call(
        _inproj_kernel,
        grid=(bsz, t_len // tm),
        in_specs=[
            pl.BlockSpec((None, tm, d), lambda b, i: (b, i, 0)),
            pl.BlockSpec((3, None, 1, d), lambda b, i: (0, b, 0, 0)),
            pl.BlockSpec((1, d), lambda b, i: (0, 0)),
            pl.BlockSpec((1, H_A * DK_A), lambda b, i: (0, 0)),
            pl.BlockSpec((tm, DK_B), lambda b, i: (i, 0)),
            pl.BlockSpec((tm, DK_B), lambda b, i: (i, 0)),
            _resident((d, N_IN), lambda b, i: (0, 0)),
        ],
        out_specs=[pl.BlockSpec((None, tm, N_IN), lambda b, i: (b, i, 0)), g_spec, g_spec],
        out_shape=[jax.ShapeDtypeStruct((bsz, t_len, N_IN), BF16), g_shape, g_shape],
        compiler_params=_params(("parallel", "parallel")),
        name="in_projection",
    )(x, mod_l, gain, lb, cos_t, sin_t, w_in_l)


def _hgrn_diag_lhs(q, k, gc, x_s, row0, l_a):
    half = SUB // 2
    for i in range(l_a // SUB):
        lo = i * SUB
        g_i, q_i = gc[lo:lo + SUB], q[lo:lo + SUB]
        pieces = []
        for s in range(SUB):
            g_s, k_s = g_i[s:s + 1], k[lo + s:lo + s + 1]
            if s < half:
                p = q_i * jnp.exp2(jnp.minimum(g_i - g_s, 0.0)) * k_s
            else:
                p = q_i[half:] * jnp.exp2(jnp.minimum(g_i[half:] - g_s, 0.0)) * k_s
                p = jnp.concatenate([jnp.zeros((half, DK_A), F32), p], axis=0)
            pieces.append(p.astype(BF16))
        x_s[row0 + lo:row0 + lo + SUB, :] = jnp.concatenate(pieces, axis=1)


def _hgrn_off_diag(q, k, gc, l_a):
    q_parts, k_parts = [], []
    for j in range(l_a // SUB - 1):
        lo, hi = j * SUB, (j + 1) * SUB
        g_ref = gc[hi - 1:hi]
        q_hat = q[hi:] * jnp.exp2(gc[hi:] - g_ref)
        k_bar = k[lo:hi] * jnp.exp2(g_ref - gc[lo:hi])
        q_parts.append(jnp.concatenate([jnp.zeros((hi, DK_A), F32), q_hat], axis=0))
        k_rows = [k_bar, jnp.zeros((l_a - hi, DK_A), F32)]
        if lo:
            k_rows.insert(0, jnp.zeros((lo, DK_A), F32))
        k_parts.append(jnp.concatenate(k_rows, axis=0))
    return _dot_nt(jnp.concatenate(q_parts, axis=1).astype(BF16),
                   jnp.concatenate(k_parts, axis=1).astype(BF16))


def _recurrence_kernel(proj_ref, ghi_ref, glo_ref, bsel_ref, gna_ref, gnb_ref, sa0_ref, sb0_ref,
                       oa_ref, ob_ref, sa_out_ref, sb_out_ref,
                       sa_s, sb_s, x_s, a_sc, qt_sc, pk_sc, *, tb, l_a, l_b):
    t = pl.program_id(1)

    @pl.when(t == 0)
    def _():
        sa_s[...] = sa0_ref[...]
        sb_s[...] = sb0_ref[...]

    ri = lax.broadcasted_iota(jnp.int32, (l_a, l_a), 0)
    ci = lax.broadcasted_iota(jnp.int32, (l_a, l_a), 1)
    tril = (ri >= ci).astype(BF16)
    diag_mask = ((ri // SUB) == (ci // SUB)) & (ri >= ci)

    def chunk_rows(c):
        return pl.ds(pl.multiple_of(c * l_a, l_a), l_a)

    def stage1_prep(c):
        rows = chunk_rows(c)
        gc_all = _dot(tril, ghi_ref[rows, :]) + _dot(tril, glo_ref[rows, :])
        q_all = proj_ref[rows, OFF_QA:OFF_QA + H_A * DK_A].astype(F32)
        k_all = proj_ref[rows, OFF_KA:OFF_KA + H_A * DK_A].astype(F32)
        return gc_all, q_all, k_all

    def stage1_head(prep, h, slot):
        cols = slice(h * DK_A, (h + 1) * DK_A)
        gc, q, k = (a[:, cols] for a in prep)
        _hgrn_diag_lhs(q, k, gc, x_s, h * l_a, l_a)
        a = _dot(x_s[h * l_a:(h + 1) * l_a, :], bsel_ref[...])
        off = _hgrn_off_diag(q, k, gc, l_a) if l_a > SUB else 0.0
        a_sc[slot, h] = jnp.where(diag_mask, a, off).astype(BF16)
        qt_sc[slot, h] = (q * jnp.exp2(gc)).astype(BF16)
        g_end = gc[l_a - 1:l_a]
        k_tail = k * jnp.exp2(g_end - gc)
        packed = jnp.concatenate(
            [k_tail, jnp.broadcast_to(jnp.exp2(g_end), (DK_A - l_a, DK_A))], axis=0)
        pk_sc[slot, h] = packed.T

    def stage2_head(c, h, slot):
        rows = chunk_rows(c)
        v = proj_ref[rows, OFF_IA + h * DV_A:OFF_IA + (h + 1) * DV_A]
        og = proj_ref[rows, OFF_OGA + h * DV_A:OFF_OGA + (h + 1) * DV_A].astype(F32)
        s_prev = sa_s[h]
        o = _dot(a_sc[slot, h], v) + _dot(qt_sc[slot, h], s_prev.astype(BF16))
        packed_t = pk_sc[slot, h]
        sa_s[h] = (s_prev * packed_t[:, l_a:l_a + 1]
                   + _dot(packed_t[:, :l_a].astype(BF16), v))
        o = _rms(o, gna_ref[h]) * og
        oa_ref[rows, h * DV_A:(h + 1) * DV_A] = o.astype(BF16)

    ti = lax.broadcasted_iota(jnp.int32, (l_b, l_b), 0)
    si = lax.broadcasted_iota(jnp.int32, (l_b, l_b), 1)
    rel = (ti - si).astype(F32)
    tcol = lax.broadcasted_iota(jnp.int32, (l_b, 1), 0).astype(F32)
    decays = [jnp.where(ti >= si, jnp.exp2(jnp.minimum(rel * lg, 0.0)), 0.0) for lg in LOG2_GAMMA]
    q_scale = [jnp.exp2((tcol + 1.0) * lg) for lg in LOG2_GAMMA]
    k_scale = [jnp.exp2((l_b - 1.0 - tcol) * lg) for lg in LOG2_GAMMA]

    def retention_head(c, h):
        rows = pl.ds(pl.multiple_of(c * l_b, l_b), l_b)
        qb = proj_ref[rows, OFF_QB + h * DK_B:OFF_QB + (h + 1) * DK_B]
        kb = proj_ref[rows, OFF_KB + h * DK_B:OFF_KB + (h + 1) * DK_B]
        v = proj_ref[rows, OFF_VB + h * DV_B:OFF_VB + (h + 1) * DV_B]
        sg = proj_ref[rows, OFF_SGB + h * DV_B:OFF_SGB + (h + 1) * DV_B].astype(F32)
        scores = _dot_nt(qb, kb) * decays[h]
        s_prev = sb_s[h]
        o = (_dot(scores.astype(BF16), v)
             + _dot((qb.astype(F32) * q_scale[h]).astype(BF16), s_prev.astype(BF16)))
        k_tail = (kb.astype(F32) * k_scale[h]).astype(BF16)
        sb_s[h] = s_prev * (2.0 ** (l_b * LOG2_GAMMA[h])) + _dot_tn(k_tail, v)
        o = _rms(o, gnb_ref[h]) * sg
        ob_ref[rows, h * DV_B:(h + 1) * DV_B] = o.astype(BF16)

    n_a = tb // l_a
    prep = stage1_prep(0)
    for h in range(H_A):
        stage1_head(prep, h, 0)
    if n_a == 1:
        for h in range(H_A):
            stage2_head(0, h, 0)
            retention_head(0, h)
    else:
        def step(s, carry):
            prep = stage1_prep(2 * s + 1)
            for h in range(H_A):
                stage1_head(prep, h, 1)
                stage2_head(2 * s, h, 0)
            prep = stage1_prep(jnp.minimum(2 * s + 2, n_a - 1))
            for h in range(H_A):
                stage1_head(prep, h, 0)
                stage2_head(2 * s + 1, h, 1)
                retention_head(s, h)
            return carry

        lax.fori_loop(0, n_a // 2, step, 0)

    @pl.when(t == pl.num_programs(1) - 1)
    def _():
        sa_out_ref[...] = sa_s[...]
        sb_out_ref[...] = sb_s[...]


def _diag_selector(l_a):
    s_of_row = jnp.arange(SUB * DK_A, dtype=jnp.int32) // DK_A
    col = jnp.arange(l_a, dtype=jnp.int32)
    return (s_of_row[:, None] == (col[None, :] % SUB)).astype(BF16)


def _recurrences(proj, g_hi, g_lo, gna, gnb, s_a0, s_b0, tb):
    bsz, t_len, _ = proj.shape
    l_a = min(64, t_len)
    l_b = min(128, t_len)
    state_spec = pl.BlockSpec((None, H_A, DK_A, DV_A), lambda b, t: (b, 0, 0, 0))
    act_spec = pl.BlockSpec((None, tb, D_MODEL), lambda b, t: (b, t, 0))
    g_spec = pl.BlockSpec((None, tb, H_A * DK_A), lambda b, t: (b, t, 0))
    return pl.pallas_call(
        functools.partial(_recurrence_kernel, tb=tb, l_a=l_a, l_b=l_b),
        grid=(bsz, t_len // tb),
        in_specs=[
            pl.BlockSpec((None, tb, REC_COLS), lambda b, t: (b, t, 0)),
            g_spec, g_spec,
            pl.BlockSpec((SUB * DK_A, l_a), lambda b, t: (0, 0)),
            pl.BlockSpec((H_A, 1, DV_A), lambda b, t: (0, 0, 0)),
            pl.BlockSpec((H_B, 1, DV_B), lambda b, t: (0, 0, 0)),
            state_spec, state_spec,
        ],
        out_specs=[act_spec, act_spec, state_spec, state_spec],
        out_shape=[
            jax.ShapeDtypeStruct((bsz, t_len, D_MODEL), BF16),
            jax.ShapeDtypeStruct((bsz, t_len, D_MODEL), BF16),
            jax.ShapeDtypeStruct((bsz, H_A, DK_A, DV_A), F32),
            jax.ShapeDtypeStruct((bsz, H_B, DK_B, DV_B), F32),
        ],
        scratch_shapes=[pltpu.VMEM((H_A, DK_A, DV_A), F32), pltpu.VMEM((H_B, DK_B, DV_B), F32),
                        pltpu.VMEM((H_A * l_a, SUB * DK_A), BF16),
                        pltpu.VMEM((2, H_A, l_a, l_a), BF16),
                        pltpu.VMEM((2, H_A, l_a, DK_A), BF16),
                        pltpu.VMEM((2, H_A, DK_A, DK_A), F32)],
        compiler_params=_params(("parallel", "arbitrary")),
        name="recurrences",
    )(proj, g_hi, g_lo, _diag_selector(l_a), gna, gnb, s_a0, s_b0)


def _merge_kernel(*refs, with_router):
    if with_router:
        (oa_ref, ob_ref, mg_ref, x_ref, mod_ref, gains_ref, wba_ref, wbb_ref, wo_ref,
         wr_ref, br_ref, lt_ref, xo_ref, h2_ref, route_ref, cnt_ref, run_s) = refs
    else:
        (oa_ref, ob_ref, mg_ref, x_ref, mod_ref, gains_ref, wba_ref, wbb_ref, wo_ref,
         xo_ref, h2_ref) = refs
    d = D_MODEL
    tm = x_ref.shape[0]
    halves = [pl.ds(0, tm // 2), pl.ds(tm // 2, tm // 2)] if tm >= 256 else [pl.ds(0, tm)]
    merged = []
    for rows in halves:
        y_a = _dot(oa_ref[rows, :], wba_ref[...])
        y_b = _dot(ob_ref[rows, :], wbb_ref[...])
        merged.append((mg_ref[rows, :d].astype(F32) * y_a
                       + mg_ref[rows, d:].astype(F32) * y_b).astype(BF16))
    m_outs = [_dot(m, wo_ref[...]) for m in merged]
    if with_router:
        w_hi, w_lo = _split_bf16(wr_ref[...])
    logit_parts = []
    for rows, m_out in zip(halves, m_outs):
        x = x_ref[rows, :] + mod_ref[2] * _rms(m_out, gains_ref[1:2])
        xo_ref[rows, :] = x
        h2 = _rms(x, gains_ref[2:3]) * (1.0 + mod_ref[4]) + mod_ref[3]
        if not with_router:
            h2_ref[rows, :] = h2.astype(BF16)
            continue
        h2_ref[rows, :] = h2
        h_hi, h_lo = _split_bf16(h2)
        logit_parts.append(_dot(h_hi, w_hi) + _dot(h_hi, w_lo) + _dot(h_lo, w_hi))
    if not with_router:
        return

    @pl.when((pl.program_id(0) == 0) & (pl.program_id(1) == 0))
    def _():
        run_s[...] = jnp.zeros_like(run_s)

    logits = jnp.concatenate(logit_parts, axis=0) + br_ref[...]
    lane = lax.broadcasted_iota(jnp.int32, logits.shape, 1)
    m1 = jnp.max(logits, axis=-1, keepdims=True)
    i1 = jnp.min(jnp.where(logits == m1, lane, N_EXPERTS), axis=-1, keepdims=True)
    rest = jnp.where(lane == i1, -jnp.inf, logits)
    m2 = jnp.max(rest, axis=-1, keepdims=True)
    i2 = jnp.min(jnp.where(rest == m2, lane, N_EXPERTS), axis=-1, keepdims=True)
    e2 = jnp.exp(m2 - m1)
    w1 = 1.0 / (1.0 + e2)
    w2 = e2 / (1.0 + e2)
    chosen = ((lane == i1) | (lane == i2)).astype(F32)
    before = run_s[...] + _dot(lt_ref[...], chosen.astype(BF16))
    r1 = jnp.sum(jnp.where(lane == i1, before, 0.0), axis=-1, keepdims=True)
    r2 = jnp.sum(jnp.where(lane == i2, before, 0.0), axis=-1, keepdims=True)
    run_s[...] = run_s[...] + jnp.sum(chosen, axis=0, keepdims=True)
    cnt_ref[...] = run_s[...]
    route = jnp.where(lane == 0, i1.astype(F32), i2.astype(F32))
    for j, val in ((2, r1), (3, r2), (4, w1), (5, w2)):
        route = jnp.where(lane == j, val, route)
    route_ref[...] = jnp.where(lane >= 6, 0.0, route)


def _merge(o_a, o_b, proj, x, mod_l, gains_l, wba, wbb, wo, router, tm):
    bsz, t_len, d = x.shape
    with_router = router is not None
    tok = pl.BlockSpec((None, tm, d), lambda b, i: (b, i, 0))
    in_specs = [
        tok, tok,
        pl.BlockSpec((None, tm, 2 * d), lambda b, i: (b, i, OFF_MGA // (2 * d))),
        tok,
        pl.BlockSpec((N_MOD, None, 1, d), lambda b, i: (0, b, 0, 0)),
        pl.BlockSpec((4, d), lambda b, i: (0, 0)),
        _resident((d, d), lambda b, i: (0, 0)),
        _resident((d, d), lambda b, i: (0, 0)),
        _resident((d, d), lambda b, i: (0, 0)),
    ]
    args = [o_a, o_b, proj, x, mod_l, gains_l, wba, wbb, wo]
    if not with_router:
        return pl.pallas_call(
            functools.partial(_merge_kernel, with_router=False),
            grid=(bsz, t_len // tm),
            in_specs=in_specs,
            out_specs=[tok, tok],
            out_shape=[jax.ShapeDtypeStruct((bsz, t_len, d), F32),
                       jax.ShapeDtypeStruct((bsz, t_len, d), BF16)],
            compiler_params=_params(("parallel", "parallel")),
            name="merge",
        )(*args)
    w_r, b_r = router
    strict_lower = jnp.tril(jnp.ones((tm, tm), BF16), -1)
    in_specs += [pl.BlockSpec((d, N_EXPERTS), lambda b, i: (0, 0)),
                 pl.BlockSpec((1, N_EXPERTS), lambda b, i: (0, 0)),
                 pl.BlockSpec((tm, tm), lambda b, i: (0, 0))]
    args += [w_r, b_r.reshape(1, N_EXPERTS), strict_lower]
    return pl.pallas_call(
        functools.partial(_merge_kernel, with_router=True),
        grid=(bsz, t_len // tm),
        in_specs=in_specs,
        out_specs=[tok, tok,
                   pl.BlockSpec((None, tm, N_EXPERTS), lambda b, i: (b, i, 0)),
                   pl.BlockSpec((1, N_EXPERTS), lambda b, i: (0, 0))],
        out_shape=[jax.ShapeDtypeStruct((bsz, t_len, d), F32),
                   jax.ShapeDtypeStruct((bsz, t_len, d), F32),
                   jax.ShapeDtypeStruct((bsz, t_len, N_EXPERTS), F32),
                   jax.ShapeDtypeStruct((1, N_EXPERTS), F32)],
        scratch_shapes=[pltpu.VMEM((1, N_EXPERTS), F32)],
        compiler_params=_params(("arbitrary", "arbitrary")),
        name="merge_router",
    )(*args)


def _swiglu_hidden(hb, wg_ref, wu_ref, a_s):
    for c in range(FFN_DIM // MXU_N):
        sl = slice(c * MXU_N, (c + 1) * MXU_N)
        gate = _dot(hb, wg_ref[:, sl])
        up = _dot(hb, wu_ref[:, sl])
        a_s[:, sl] = (gate * _sigmoid(gate) * up).astype(BF16)


def _ffn_dense_kernel(h_ref, wg_ref, wu_ref, wd_ref, x_ref, mod_ref, gain_ref, o_ref, a_s):
    _swiglu_hidden(h_ref[...], wg_ref, wu_ref, a_s)
    y = _dot(a_s[...], wd_ref[...])
    o_ref[...] = x_ref[...] + mod_ref[0] * _rms(y, gain_ref[...])


def _ffn_dense(h2, wg, wu, wd, x, mod_l, gain, tm):
    bsz, t_len, d = x.shape
    tok = pl.BlockSpec((None, tm, d), lambda b, i: (b, i, 0))
    return pl.pallas_call(
        _ffn_dense_kernel,
        grid=(bsz, t_len // tm),
        in_specs=[
            tok,
            _resident((d, FFN_DIM), lambda b, i: (0, 0)),
            _resident((d, FFN_DIM), lambda b, i: (0, 0)),
            _resident((FFN_DIM, d), lambda b, i: (0, 0)),
            tok,
            pl.BlockSpec((1, None, 1, d), lambda b, i: (N_MOD - 1, b, 0, 0)),
            pl.BlockSpec((1, d), lambda b, i: (0, 0)),
        ],
        out_specs=tok,
        out_shape=jax.ShapeDtypeStruct((bsz, t_len, d), F32),
        scratch_shapes=[pltpu.VMEM((tm, FFN_DIM), BF16)],
        compiler_params=_params(("parallel", "parallel")),
        name="ffn_dense",
    )(h2, wg, wu, wd, x, mod_l, gain)


def _ffn_grouped_kernel(te_ref, nu_ref, x_ref, wg_ref, wu_ref, wd_ref, y_ref, a_s):
    del te_ref

    @pl.when(pl.program_id(0) < nu_ref[0])
    def _():
        tmg = x_ref.shape[0]
        x = x_ref[...].reshape(tmg, D_MODEL)
        _swiglu_hidden(x.astype(BF16), wg_ref, wu_ref, a_s)
        y_ref[...] = _dot(a_s[...], wd_ref[...]).reshape(y_ref.shape)


def _ffn_grouped(x_sorted, tile_expert, n_used, wg, wu, wd, tmg):
    n_slots = x_sorted.shape[0]
    d = D_MODEL
    n_tiles = n_slots // tmg

    def row_map(i, te, nu):
        return (jnp.minimum(i, nu[0] - 1), 0, 0)

    def w_map(i, te, nu):
        return (te[i], 0, 0)

    return pl.pallas_call(
        _ffn_grouped_kernel,
        grid_spec=pltpu.PrefetchScalarGridSpec(
            num_scalar_prefetch=2,
            grid=(n_tiles,),
            in_specs=[
                pl.BlockSpec((tmg,) + ROW_TILE, row_map),
                pl.BlockSpec((None, d, FFN_DIM), w_map),
                pl.BlockSpec((None, d, FFN_DIM), w_map),
                pl.BlockSpec((None, FFN_DIM, d), w_map),
            ],
            out_specs=pl.BlockSpec((tmg,) + ROW_TILE, row_map),
            scratch_shapes=[pltpu.VMEM((tmg, FFN_DIM), BF16)],
        ),
        out_shape=jax.ShapeDtypeStruct((n_slots,) + ROW_TILE, F32),
        compiler_params=_params(("arbitrary",)),
        name="ffn_grouped",
    )(tile_expert, n_used, x_sorted, wg, wu, wd)


def _row_copy(src_ref, dst_ref, sem):
    return pltpu.make_async_copy(src_ref, dst_ref, sem)


def _start_rows(copies, n_rows):
    def issue(r, carry):
        for i, cp in enumerate(copies(r)):
            cp.start(priority=i % 2)
        return carry

    lax.fori_loop(0, n_rows, issue, 0, unroll=8)


def _wait_rows(copies, n_rows):
    def drain(r, carry):
        for cp in copies(r):
            cp.wait()
        return carry

    lax.fori_loop(0, n_rows, drain, 0, unroll=8)


def _scatter_kernel(s1_ref, s2_ref, p1_ref, p2_ref, ztile_ref, h_ref, xs_ref,
                    rows_s, zero_s, sems, zsem):
    tm = h_ref.shape[0]
    tmg = zero_s.shape[0]
    i = pl.program_id(0)
    par = i % 2

    @pl.when(i == 0)
    def _():
        zero_s[...] = jnp.zeros_like(zero_s)

        def zero_copy(e):
            return pltpu.make_async_copy(zero_s, xs_ref.at[pl.ds(ztile_ref[e], tmg)], zsem)

        for e in range(N_EXPERTS):
            pl.when(ztile_ref[e] >= 0)(lambda e=e: zero_copy(e).start())
        for e in range(N_EXPERTS):
            pl.when(ztile_ref[e] >= 0)(lambda e=e: zero_copy(e).wait())

    def row_copies(slot_a, slot_b, p):
        def copies(r):
            src = rows_s.at[p, pl.ds(r, 1)]
            return (_row_copy(src, xs_ref.at[pl.ds(slot_a[0, r], 1)], sems.at[p]),
                    _row_copy(src, xs_ref.at[pl.ds(slot_b[0, r], 1)], sems.at[p]))
        return copies

    rows_s[par] = h_ref[...].reshape(rows_s.shape[1:])
    _start_rows(row_copies(s1_ref, s2_ref, par), tm)
    pl.when(i > 0)(lambda: _wait_rows(row_copies(p1_ref, p2_ref, 1 - par), tm))
    pl.when(i == pl.num_programs(0) - 1)(lambda: _wait_rows(row_copies(s1_ref, s2_ref, par), tm))


def _scatter_rows(h2, slot1, slot2, zero_tile_row, n_slots, tm, tmg):
    bsz, t_len, d = h2.shape
    n = bsz * t_len // tm

    def slot_spec(shift):
        return pl.BlockSpec((None, 1, tm), lambda i: (jnp.clip(i + shift, 0, n - 1), 0, 0),
                            memory_space=pltpu.SMEM)

    return pl.pallas_call(
        _scatter_kernel,
        grid=(n,),
        in_specs=[slot_spec(0), slot_spec(0), slot_spec(-1), slot_spec(-1),
                  pl.BlockSpec(memory_space=pltpu.SMEM),
                  pl.BlockSpec((tm, d), lambda i: (i, 0))],
        out_specs=pl.BlockSpec(memory_space=pl.ANY),
        out_shape=jax.ShapeDtypeStruct((n_slots,) + ROW_TILE, F32),
        scratch_shapes=[pltpu.VMEM((2, tm) + ROW_TILE, F32), pltpu.VMEM((tmg,) + ROW_TILE, F32),
                        pltpu.SemaphoreType.DMA((2,)), pltpu.SemaphoreType.DMA],
        compiler_params=_params(("arbitrary",)),
        name="scatter_rows",
    )(slot1, slot2, slot1, slot2, zero_tile_row, h2.reshape(bsz * t_len, d))


def _combine_kernel(s1_ref, s2_ref, n1_ref, n2_ref, route_ref, x_ref, mod_ref, gain_ref, y_ref,
                    o_ref, buf1, buf2, sems):
    tm, d = x_ref.shape
    i = pl.program_id(0)
    par = i % 2

    def row_copies(slot_a, slot_b, p):
        def copies(r):
            return (_row_copy(y_ref.at[pl.ds(slot_a[0, r], 1)], buf1.at[p, pl.ds(r, 1)], sems.at[p]),
                    _row_copy(y_ref.at[pl.ds(slot_b[0, r], 1)], buf2.at[p, pl.ds(r, 1)], sems.at[p]))
        return copies

    pl.when(i == 0)(lambda: _start_rows(row_copies(s1_ref, s2_ref, par), tm))
    pl.when(i + 1 < pl.num_programs(0))(
        lambda: _start_rows(row_copies(n1_ref, n2_ref, 1 - par), tm))
    _wait_rows(row_copies(s1_ref, s2_ref, par), tm)
    f_out = (route_ref[:, 4:5] * buf1[par].reshape(tm, d)
             + route_ref[:, 5:6] * buf2[par].reshape(tm, d))
    o_ref[...] = x_ref[...] + mod_ref[0] * _rms(f_out, gain_ref[...])


def _combine_rows(y_sorted, slot1, slot2, route, x, mod_l, gain, tm):
    bsz, t_len, d = x.shape
    nt = t_len // tm
    n = bsz * nt

    def slot_spec(shift):
        return pl.BlockSpec((None, 1, tm), lambda i: (jnp.minimum(i + shift, n - 1), 0, 0),
                            memory_space=pltpu.SMEM)

    tok = pl.BlockSpec((None, tm, d), lambda i: (i // nt, i % nt, 0))
    return pl.pallas_call(
        _combine_kernel,
        grid=(n,),
        in_specs=[slot_spec(0), slot_spec(0), slot_spec(1), slot_spec(1),
                  pl.BlockSpec((None, tm, N_EXPERTS), lambda i: (i // nt, i % nt, 0)),
                  tok,
                  pl.BlockSpec((1, None, 1, d), lambda i: (N_MOD - 1, i // nt, 0, 0)),
                  pl.BlockSpec((1, d), lambda i: (0, 0)),
                  pl.BlockSpec(memory_space=pl.ANY)],
        out_specs=tok,
        out_shape=jax.ShapeDtypeStruct((bsz, t_len, d), F32),
        scratch_shapes=[pltpu.VMEM((2, tm) + ROW_TILE, F32), pltpu.VMEM((2, tm) + ROW_TILE, F32),
                        pltpu.SemaphoreType.DMA((2,))],
        compiler_params=_params(("arbitrary",)),
        name="combine_rows",
    )(slot1, slot2, slot1, slot2, route, x, mod_l, gain, y_sorted)


def _moe_ffn(h2, route, counts, wg, wu, wd, x, mod_l, gain, tm, tmg):
    bsz, t_len, d = x.shape
    n_tok = bsz * t_len
    n_tiles = (2 * n_tok) // tmg + N_EXPERTS
    e1 = route[..., 0].astype(jnp.int32).reshape(n_tok)
    e2 = route[..., 1].astype(jnp.int32).reshape(n_tok)
    r1 = route[..., 2].astype(jnp.int32).reshape(n_tok)
    r2 = route[..., 3].astype(jnp.int32).reshape(n_tok)
    cnt = counts[0].astype(jnp.int32)
    tiles_per = (cnt + tmg - 1) // tmg
    tile_end = jnp.cumsum(tiles_per)
    offs = (tile_end - tiles_per) * tmg
    n_used = tile_end[-1:]
    tile_ids = jnp.arange(n_tiles, dtype=jnp.int32)
    tile_expert = jnp.sum(tile_ids[:, None] >= tile_end[None, :], axis=1).astype(jnp.int32)
    last_expert = jnp.sum(n_used - 1 >= tile_end).astype(jnp.int32)
    tile_expert = jnp.where(tile_ids < n_used, tile_expert, last_expert)
    zero_tile_row = jnp.where(cnt > 0, (tile_end - 1) * tmg, -1).astype(jnp.int32)
    nt = t_len // tm
    slot1 = (offs[e1] + r1).reshape(bsz * nt, 1, tm)
    slot2 = (offs[e2] + r2).reshape(bsz * nt, 1, tm)
    x_sorted = _scatter_rows(h2, slot1, slot2, zero_tile_row, n_tiles * tmg, tm, tmg)
    y_sorted = _ffn_grouped(x_sorted, tile_expert, n_used, wg, wu, wd, tmg)
    return _combine_rows(y_sorted, slot1, slot2, route, x, mod_l, gain, tm)


def _rope_tables(pos):
    half = DK_B // 2
    inv = ROPE_BASE ** (-jnp.arange(half, dtype=F32) / half)
    ang = pos.astype(F32)[:, None] * inv[None, :]
    cos, sin = jnp.cos(ang), jnp.sin(ang)
    return jnp.concatenate([cos, cos], axis=-1), jnp.concatenate([-sin, sin], axis=-1)


def _trunk(x, mod, pos, s_a_in, s_b_in, lower_bounds, wts, tm, tb, tmg):
    depth = mod.shape[0]
    bsz = x.shape[0]
    cos_t, sin_t = _rope_tables(pos)
    sa_out, sb_out = [], []
    for l in range(depth):
        mod_l = mod[l]
        gains_l = wts["norm_gains"][l]
        if s_a_in is None:
            s_a = jnp.zeros((bsz, H_A, DK_A, DV_A), F32)
            s_b = jnp.zeros((bsz, H_B, DK_B, DV_B), F32)
        else:
            s_a, s_b = s_a_in[l], s_b_in[l]
        proj, g_hi, g_lo = _in_projection(x, mod_l, gains_l[0:1], lower_bounds[l:l + 1],
                                          cos_t, sin_t, wts["w_in"][l], tm)
        o_a, o_b, s_a_new, s_b_new = _recurrences(
            proj, g_hi, g_lo,
            wts["g_norm_a"][l].reshape(H_A, 1, DV_A), wts["g_norm_b"][l].reshape(H_B, 1, DV_B),
            s_a, s_b, tb)
        j = l // 2
        router = None if l % 2 == 0 else (wts["w_router"][j], wts["b_router"][j])
        outs = _merge(o_a, o_b, proj, x, mod_l, gains_l, wts["w_branch_a"][l],
                      wts["w_branch_b"][l], wts["w_out"][l], router, tm)
        if l % 2 == 0:
            x, h2 = outs
            x = _ffn_dense(h2, wts["w_gate_dense"][j], wts["w_up_dense"][j],
                           wts["w_down_dense"][j], x, mod_l, gains_l[3:4], tm)
        else:
            x, h2, route, counts = outs
            x = _moe_ffn(h2, route, counts, wts["w_gate_exp"][j], wts["w_up_exp"][j],
                         wts["w_down_exp"][j], x, mod_l, gains_l[3:4], tm, tmg)
        sa_out.append(s_a_new)
        sb_out.append(s_b_new)
    return x, jnp.stack(sa_out), jnp.stack(sb_out)


def kernel(x_prompt, x_sample, state_hgrn, state_ret, c_prompt, c_sample, w_mod, b_mod, norm_gains,
           w_in, lb_logits, g_norm_a, g_norm_b, w_branch_a, w_branch_b, w_out, w_gate_dense,
           w_up_dense, w_down_dense, w_router, b_router, w_gate_exp, w_up_exp, w_down_exp):
    n_prompt = x_prompt.shape[0]
    past_len = 1024
    wts = dict(
        norm_gains=norm_gains, g_norm_a=g_norm_a, g_norm_b=g_norm_b,
        w_router=w_router, b_router=b_router,
        w_in=w_in.astype(BF16), w_branch_a=w_branch_a.astype(BF16),
        w_branch_b=w_branch_b.astype(BF16), w_out=w_out.astype(BF16),
        w_gate_dense=w_gate_dense.astype(BF16), w_up_dense=w_up_dense.astype(BF16),
        w_down_dense=w_down_dense.astype(BF16), w_gate_exp=w_gate_exp.astype(BF16),
        w_up_exp=w_up_exp.astype(BF16), w_down_exp=w_down_exp.astype(BF16))
    lb_p = jax.nn.softmax(lb_logits.astype(F32), axis=0)
    lower_bounds = jnp.clip(jnp.cumsum(lb_p, axis=0) - lb_p[:1], 0.0, 1.0)
    c_all = jnp.concatenate([c_prompt, c_sample], axis=0)
    mod = _modulation(c_all, w_mod, b_mod)
    mod = mod.reshape(mod.shape[:3] + (1, D_MODEL))
    pos_prompt = jnp.arange(x_prompt.shape[1], dtype=jnp.int32)
    pos_sample = past_len + jnp.arange(x_sample.shape[1], dtype=jnp.int32)
    t_p, t_s = x_prompt.shape[1], x_sample.shape[1]
    y_p, hg_p, rt_p = _trunk(x_prompt, mod[:, :, :n_prompt], pos_prompt, None, None,
                             lower_bounds, wts, tm=min(512, t_p), tb=min(1024, t_p), tmg=512)
    y_s, hg_s, rt_s = _trunk(x_sample, mod[:, :, n_prompt:], pos_sample, state_hgrn, state_ret,
                             lower_bounds, wts, tm=min(512, t_s), tb=min(512, t_s), tmg=64)
    return (y_p, y_s, hg_p, rt_p, hg_s, rt_s)
```

```python
import functools
import math

import jax
import jax.numpy as jnp
from jax import lax
from jax.experimental import pallas as pl
from jax.experimental.pallas import tpu as pltpu

F32 = jnp.float32
BF16 = jnp.bfloat16

D_MODEL = 1024
H_A, DK_A, DV_A = 4, 128, 256
H_B, DK_B, DV_B = 4, 128, 256
ROPE_BASE = 10000.0
FFN_DIM = 2816
N_EXPERTS = 8
RMS_EPS = 1e-6
F_FLOOR = 1e-20
N_MOD = 6
N_IN = 8 * D_MODEL
OFF_QA, OFF_KA, OFF_IA, OFF_OGA = 0, 512, 1024, 2048
OFF_QB, OFF_KB, OFF_VB, OFF_SGB = 3072, 3584, 4096, 5120
OFF_MGA, OFF_MGB = 6144, 7168
REC_COLS = 6144
SUB = 8
LOG2_GAMMA = tuple(math.log2(1.0 - 2.0 ** (-5.0 - h)) for h in range(H_B))
VMEM_LIMIT = 56 * 1024 * 1024
MXU_N = 256
ROW_TILE = (8, 128)


def _params(sem):
    return pltpu.CompilerParams(dimension_semantics=sem, vmem_limit_bytes=VMEM_LIMIT)


def _resident(shape, index_map):
    return pl.BlockSpec(shape, index_map, pipeline_mode=pl.Buffered(1))


def _rms(x, gain):
    return x * lax.rsqrt(jnp.mean(x * x, axis=-1, keepdims=True) + RMS_EPS) * gain


def _sigmoid(x):
    return 1.0 / (1.0 + jnp.exp(-x))


def _dot(a, b):
    return jnp.dot(a, b, preferred_element_type=F32)


def _dot_nt(a, b):
    return lax.dot_general(a, b, (((1,), (1,)), ((), ())), preferred_element_type=F32)


def _dot_tn(a, b):
    return lax.dot_general(a, b, (((0,), (0,)), ((), ())), preferred_element_type=F32)


def _split_bf16(x):
    hi = x.astype(BF16)
    return hi, (x - hi.astype(F32)).astype(BF16)


def _mod_kernel(c_ref, w_ref, b_ref, o_ref):
    c = c_ref[...]
    c_act = (c * _sigmoid(c)).astype(BF16)
    o_ref[...] = _dot(c_act, w_ref[...].astype(BF16)) + b_ref[...]


def _modulation(c_all, w_mod, b_mod):
    depth = w_mod.shape[0]
    n_seq = c_all.shape[0]
    d = D_MODEL
    return pl.pallas_call(
        _mod_kernel,
        grid=(depth, N_MOD),
        in_specs=[
            pl.BlockSpec((n_seq, d), lambda l, j: (0, 0)),
            pl.BlockSpec((None, d, d), lambda l, j: (l, 0, j)),
            pl.BlockSpec((None, None, 1, d), lambda l, j: (l, j, 0, 0)),
        ],
        out_specs=pl.BlockSpec((None, None, n_seq, d), lambda l, j: (l, j, 0, 0)),
        out_shape=jax.ShapeDtypeStruct((depth, N_MOD, n_seq, d), F32),
        compiler_params=_params(("parallel", "parallel")),
        name="modulation",
    )(c_all, w_mod, b_mod.reshape(depth, N_MOD, 1, d))


def _inproj_kernel(x_ref, mod_ref, gain_ref, lb_ref, cos_ref, sin_ref, w_ref,
                   o_ref, ghi_ref, glo_ref):
    d = D_MODEL
    h = _rms(x_ref[...], gain_ref[...]) * (1.0 + mod_ref[1]) + mod_ref[0]
    hb = h.astype(BF16)

    def seg(cols):
        return _dot(hb, w_ref[:, cols[0]:cols[0] + cols[1]])

    def hgrn_q(q):
        o_ref[:, OFF_QA:OFF_QA + 512] = (q * _sigmoid(q)).astype(BF16)

    def hgrn_f(f_raw):
        lb = lb_ref[...]
        sig = _sigmoid(f_raw)
        g2 = jnp.log2(jnp.maximum(lb + (1.0 - lb) * sig, F_FLOOR))
        ghi, glo = _split_bf16(g2)
        ghi_ref[...] = ghi
        glo_ref[...] = glo
        o_ref[:, OFF_KA:OFF_KA + 512] = ((1.0 - lb) * (1.0 - sig)).astype(BF16)

    def retention_qk(y):
        cos, sin = cos_ref[...], sin_ref[...]
        for hh in range(H_B):
            xq = y[:, hh * DK_B:(hh + 1) * DK_B]
            xk = y[:, 512 + hh * DK_B:512 + (hh + 1) * DK_B]
            rq = xq * cos + pltpu.roll(xq, DK_B // 2, axis=1) * sin
            rk = (xk * cos + pltpu.roll(xk, DK_B // 2, axis=1) * sin) * (DK_B ** -0.5)
            o_ref[:, OFF_QB + hh * DK_B:OFF_QB + (hh + 1) * DK_B] = rq.astype(BF16)
            o_ref[:, OFF_KB + hh * DK_B:OFF_KB + (hh + 1) * DK_B] = rk.astype(BF16)

    def store(off, act):
        def epilogue(y):
            o_ref[:, off:off + d] = act(y).astype(BF16)
        return epilogue

    plan = [((1 * d, d), store(OFF_IA, lambda y: y)), ((512, 512), hgrn_f),
            ((2 * d, d), store(OFF_OGA, _sigmoid)), ((0, 512), hgrn_q),
            ((3 * d, d), retention_qk), ((5 * d, d), store(OFF_SGB, lambda y: y * _sigmoid(y))),
            ((6 * d, d), store(OFF_MGA, _sigmoid)), ((7 * d, d), store(OFF_MGB, _sigmoid)),
            ((4 * d, d), store(OFF_VB, lambda y: y))]
    y = seg(plan[0][0])
    for i, (_, epilogue) in enumerate(plan):
        y_next = seg(plan[i + 1][0]) if i + 1 < len(plan) else None
        epilogue(y)
        y = y_next


def _in_projection(x, mod_l, gain, lb, cos_t, sin_t, w_in_l, tm):
    bsz, t_len, d = x.shape
    g_spec = pl.BlockSpec((None, tm, H_A * DK_A), lambda b, i: (b, i, 0))
    g_shape = jax.ShapeDtypeStruct((bsz, t_len, H_A * DK_A), BF16)
    return pl.pallas_call(
        _inproj_kernel,
        grid=(bsz, t_len // tm),
        in_specs=[
            pl.BlockSpec((None, tm, d), lambda b, i: (b, i, 0)),
            pl.BlockSpec((3, None, 1, d), lambda b, i: (0, b, 0, 0)),
            pl.BlockSpec((1, d), lambda b, i: (0, 0)),
            pl.BlockSpec((1, H_A * DK_A), lambda b, i: (0, 0)),
            pl.BlockSpec((tm, DK_B), lambda b, i: (i, 0)),
            pl.BlockSpec((tm, DK_B), lambda b, i: (i, 0)),
            _resident((d, N_IN), lambda b, i: (0, 0)),
        ],
        out_specs=[pl.BlockSpec((None, tm, N_IN), lambda b, i: (b, i, 0)), g_spec, g_spec],
        out_shape=[jax.ShapeDtypeStruct((bsz, t_len, N_IN), BF16), g_shape, g_shape],
        compiler_params=_params(("parallel", "parallel")),
        name="in_projection",
    )(x, mod_l, gain, lb, cos_t, sin_t, w_in_l)


def _hgrn_diag_lhs(q, k, gc, x_s, row0, l_a):
    pack = 2 * SUB
    for m in range(l_a // pack):
        pieces = []
        for s in range(SUB):
            halves = []
            for lo in (m * pack, m * pack + SUB):
                decay = jnp.exp2(jnp.minimum(gc[lo:lo + SUB] - gc[lo + s:lo + s + 1], 0.0))
                halves.append(q[lo:lo + SUB] * decay * k[lo + s:lo + s + 1])
            pieces.append(jnp.concatenate(halves, axis=0).astype(BF16))
        x_s[row0 + m * pack:row0 + (m + 1) * pack, :] = jnp.concatenate(pieces, axis=1)


def _hgrn_off_diag(q, k, gc, l_a):
    q_parts, k_parts = [], []
    for j in range(l_a // SUB - 1):
        lo, hi = j * SUB, (j + 1) * SUB
        g_ref = gc[hi - 1:hi]
        q_hat = q[hi:] * jnp.exp2(gc[hi:] - g_ref)
        k_bar = k[lo:hi] * jnp.exp2(g_ref - gc[lo:hi])
        q_parts.append(jnp.concatenate([jnp.zeros((hi, DK_A), F32), q_hat], axis=0))
        k_rows = [k_bar, jnp.zeros((l_a - hi, DK_A), F32)]
        if lo:
            k_rows.insert(0, jnp.zeros((lo, DK_A), F32))
        k_parts.append(jnp.concatenate(k_rows, axis=0))
    return _dot_nt(jnp.concatenate(q_parts, axis=1).astype(BF16),
                   jnp.concatenate(k_parts, axis=1).astype(BF16))


def _recurrence_kernel(proj_ref, ghi_ref, glo_ref, bsel_ref, gna_ref, gnb_ref, sa0_ref, sb0_ref,
                       sa_all_ref, sb_all_ref, oa_ref, ob_ref, sa_out_ref, sb_out_ref,
                       sa_s, sb_s, x_s, a_sc, qt_sc, pk_sc, *, tb, l_a, l_b):
    del sa_all_ref, sb_all_ref
    t = pl.program_id(1)

    @pl.when(t == 0)
    def _():
        sa_s[...] = sa0_ref[...]
        sb_s[...] = sb0_ref[...]

    ri = lax.broadcasted_iota(jnp.int32, (l_a, l_a), 0)
    ci = lax.broadcasted_iota(jnp.int32, (l_a, l_a), 1)
    tril = (ri >= ci).astype(BF16)
    diag_mask = ((ri // SUB) == (ci // SUB)) & (ri >= ci)

    def chunk_rows(c):
        return pl.ds(pl.multiple_of(c * l_a, l_a), l_a)

    def stage1_prep(c):
        rows = chunk_rows(c)
        gc_all = _dot(tril, ghi_ref[rows, :]) + _dot(tril, glo_ref[rows, :])
        q_all = proj_ref[rows, OFF_QA:OFF_QA + H_A * DK_A].astype(F32)
        k_all = proj_ref[rows, OFF_KA:OFF_KA + H_A * DK_A].astype(F32)
        return gc_all, q_all, k_all

    def stage1_head(prep, h, slot):
        cols = slice(h * DK_A, (h + 1) * DK_A)
        gc, q, k = (a[:, cols] for a in prep)
        _hgrn_diag_lhs(q, k, gc, x_s, h * l_a, l_a)
        a = _dot(x_s[h * l_a:(h + 1) * l_a, :], bsel_ref[...])
        off = _hgrn_off_diag(q, k, gc, l_a) if l_a > SUB else 0.0
        a_sc[slot, h] = jnp.where(diag_mask, a, off).astype(BF16)
        qt_sc[slot, h] = (q * jnp.exp2(gc)).astype(BF16)
        g_end = gc[l_a - 1:l_a]
        k_tail = k * jnp.exp2(g_end - gc)
        packed = jnp.concatenate(
            [k_tail, jnp.broadcast_to(jnp.exp2(g_end), (DK_A - l_a, DK_A))], axis=0)
        pk_sc[slot, h] = packed.T

    def stage2_head(c, h, slot):
        rows = chunk_rows(c)
        v = proj_ref[rows, OFF_IA + h * DV_A:OFF_IA + (h + 1) * DV_A]
        og = proj_ref[rows, OFF_OGA + h * DV_A:OFF_OGA + (h + 1) * DV_A].astype(F32)
        s_prev = sa_s[h]
        o = _dot(a_sc[slot, h], v) + _dot(qt_sc[slot, h], s_prev.astype(BF16))
        packed_t = pk_sc[slot, h]
        sa_s[h] = (s_prev * packed_t[:, l_a:l_a + 1]
                   + _dot(packed_t[:, :l_a].astype(BF16), v))
        o = _rms(o, gna_ref[h]) * og
        oa_ref[rows, h * DV_A:(h + 1) * DV_A] = o.astype(BF16)

    ti = lax.broadcasted_iota(jnp.int32, (l_b, l_b), 0)
    si = lax.broadcasted_iota(jnp.int32, (l_b, l_b), 1)
    rel = (ti - si).astype(F32)
    tcol = lax.broadcasted_iota(jnp.int32, (l_b, 1), 0).astype(F32)
    decays = [jnp.where(ti >= si, jnp.exp2(jnp.minimum(rel * lg, 0.0)), 0.0) for lg in LOG2_GAMMA]
    q_scale = [jnp.exp2((tcol + 1.0) * lg) for lg in LOG2_GAMMA]
    k_scale = [jnp.exp2((l_b - 1.0 - tcol) * lg) for lg in LOG2_GAMMA]

    def retention_head(c, h):
        rows = pl.ds(pl.multiple_of(c * l_b, l_b), l_b)
        qb = proj_ref[rows, OFF_QB + h * DK_B:OFF_QB + (h + 1) * DK_B]
        kb = proj_ref[rows, OFF_KB + h * DK_B:OFF_KB + (h + 1) * DK_B]
        v = proj_ref[rows, OFF_VB + h * DV_B:OFF_VB + (h + 1) * DV_B]
        sg = proj_ref[rows, OFF_SGB + h * DV_B:OFF_SGB + (h + 1) * DV_B].astype(F32)
        scores = _dot_nt(qb, kb) * decays[h]
        s_prev = sb_s[h]
        o = (_dot(scores.astype(BF16), v)
             + _dot((qb.astype(F32) * q_scale[h]).astype(BF16), s_prev.astype(BF16)))
        k_tail = (kb.astype(F32) * k_scale[h]).astype(BF16)
        sb_s[h] = s_prev * (2.0 ** (l_b * LOG2_GAMMA[h])) + _dot_tn(k_tail, v)
        o = _rms(o, gnb_ref[h]) * sg
        ob_ref[rows, h * DV_B:(h + 1) * DV_B] = o.astype(BF16)

    n_a = tb // l_a
    prep = stage1_prep(0)
    for h in range(H_A):
        stage1_head(prep, h, 0)
    if n_a == 1:
        for h in range(H_A):
            stage2_head(0, h, 0)
            retention_head(0, h)
    else:
        def step(s, carry):
            prep = stage1_prep(2 * s + 1)
            for h in range(H_A):
                stage1_head(prep, h, 1)
                stage2_head(2 * s, h, 0)
            prep = stage1_prep(jnp.minimum(2 * s + 2, n_a - 1))
            for h in range(H_A):
                stage1_head(prep, h, 0)
                stage2_head(2 * s + 1, h, 1)
                retention_head(s, h)
            return carry

        lax.fori_loop(0, n_a // 2, step, 0)

    @pl.when(t == pl.num_programs(1) - 1)
    def _():
        sa_out_ref[...] = sa_s[...]
        sb_out_ref[...] = sb_s[...]


def _diag_selector(l_a):
    s_of_row = jnp.arange(SUB * DK_A, dtype=jnp.int32) // DK_A
    col = jnp.arange(l_a, dtype=jnp.int32)
    return (s_of_row[:, None] == (col[None, :] % SUB)).astype(BF16)


def _recurrences(proj, g_hi, g_lo, gna, gnb, s_a0, s_b0, sa_all, sb_all, layer, tb):
    bsz, t_len, _ = proj.shape
    l_a = min(64, t_len)
    l_b = min(128, t_len)
    state_spec = pl.BlockSpec((None, H_A, DK_A, DV_A), lambda b, t: (b, 0, 0, 0))
    state_out_spec = pl.BlockSpec((None, None, H_A, DK_A, DV_A), lambda b, t: (layer, b, 0, 0, 0))
    hbm = pl.BlockSpec(memory_space=pl.ANY)
    act_spec = pl.BlockSpec((None, tb, D_MODEL), lambda b, t: (b, t, 0))
    g_spec = pl.BlockSpec((None, tb, H_A * DK_A), lambda b, t: (b, t, 0))
    return pl.pallas_call(
        functools.partial(_recurrence_kernel, tb=tb, l_a=l_a, l_b=l_b),
        grid=(bsz, t_len // tb),
        in_specs=[
            pl.BlockSpec((None, tb, REC_COLS), lambda b, t: (b, t, 0)),
            g_spec, g_spec,
            pl.BlockSpec((SUB * DK_A, l_a), lambda b, t: (0, 0)),
            pl.BlockSpec((H_A, 1, DV_A), lambda b, t: (0, 0, 0)),
            pl.BlockSpec((H_B, 1, DV_B), lambda b, t: (0, 0, 0)),
            state_spec, state_spec, hbm, hbm,
        ],
        out_specs=[act_spec, act_spec, state_out_spec, state_out_spec],
        out_shape=[
            jax.ShapeDtypeStruct((bsz, t_len, D_MODEL), BF16),
            jax.ShapeDtypeStruct((bsz, t_len, D_MODEL), BF16),
            jax.ShapeDtypeStruct(sa_all.shape, F32),
            jax.ShapeDtypeStruct(sb_all.shape, F32),
        ],
        input_output_aliases={8: 2, 9: 3},
        scratch_shapes=[pltpu.VMEM((H_A, DK_A, DV_A), F32), pltpu.VMEM((H_B, DK_B, DV_B), F32),
                        pltpu.VMEM((H_A * l_a, SUB * DK_A), BF16),
                        pltpu.VMEM((2, H_A, l_a, l_a), BF16),
                        pltpu.VMEM((2, H_A, l_a, DK_A), BF16),
                        pltpu.VMEM((2, H_A, DK_A, DK_A), F32)],
        compiler_params=_params(("parallel", "arbitrary")),
        name="recurrences",
    )(proj, g_hi, g_lo, _diag_selector(l_a), gna, gnb, s_a0, s_b0, sa_all, sb_all)


def _merge_kernel(*refs, with_router):
    if with_router:
        (oa_ref, ob_ref, mg_ref, x_ref, mod_ref, gains_ref, wba_ref, wbb_ref, wo_ref,
         wr_ref, br_ref, lt_ref, xo_ref, h2_ref, route_ref, cnt_ref, run_s) = refs
    else:
        (oa_ref, ob_ref, mg_ref, x_ref, mod_ref, gains_ref, wba_ref, wbb_ref, wo_ref,
         xo_ref, h2_ref) = refs
    d = D_MODEL
    tm = x_ref.shape[0]
    halves = [pl.ds(0, tm // 2), pl.ds(tm // 2, tm // 2)] if tm >= 256 else [pl.ds(0, tm)]
    merged = []
    for rows in halves:
        y_a = _dot(oa_ref[rows, :], wba_ref[...])
        y_b = _dot(ob_ref[rows, :], wbb_ref[...])
        merged.append((mg_ref[rows, :d].astype(F32) * y_a
                       + mg_ref[rows, d:].astype(F32) * y_b).astype(BF16))
    m_outs = [_dot(m, wo_ref[...]) for m in merged]
    if with_router:
        w_hi, w_lo = _split_bf16(wr_ref[...])
    logit_parts = []
    for rows, m_out in zip(halves, m_outs):
        x = x_ref[rows, :] + mod_ref[2] * _rms(m_out, gains_ref[1:2])
        xo_ref[rows, :] = x
        h2 = _rms(x, gains_ref[2:3]) * (1.0 + mod_ref[4]) + mod_ref[3]
        if not with_router:
            h2_ref[rows, :] = h2.astype(BF16)
            continue
        h2_ref[rows, :] = h2
        h_hi, h_lo = _split_bf16(h2)
        logit_parts.append(_dot(h_hi, w_hi) + _dot(h_hi, w_lo) + _dot(h_lo, w_hi))
    if not with_router:
        return

    @pl.when((pl.program_id(0) == 0) & (pl.program_id(1) == 0))
    def _():
        run_s[...] = jnp.zeros_like(run_s)

    logits = jnp.concatenate(logit_parts, axis=0) + br_ref[...]
    lane = lax.broadcasted_iota(jnp.int32, logits.shape, 1)
    m1 = jnp.max(logits, axis=-1, keepdims=True)
    i1 = jnp.min(jnp.where(logits == m1, lane, N_EXPERTS), axis=-1, keepdims=True)
    rest = jnp.where(lane == i1, -jnp.inf, logits)
    m2 = jnp.max(rest, axis=-1, keepdims=True)
    i2 = jnp.min(jnp.where(rest == m2, lane, N_EXPERTS), axis=-1, keepdims=True)
    e2 = jnp.exp(m2 - m1)
    w1 = 1.0 / (1.0 + e2)
    w2 = e2 / (1.0 + e2)
    chosen = ((lane == i1) | (lane == i2)).astype(F32)
    before = run_s[...] + _dot(lt_ref[...], chosen.astype(BF16))
    r1 = jnp.sum(jnp.where(lane == i1, before, 0.0), axis=-1, keepdims=True)
    r2 = jnp.sum(jnp.where(lane == i2, before, 0.0), axis=-1, keepdims=True)
    run_s[...] = run_s[...] + jnp.sum(chosen, axis=0, keepdims=True)
    cnt_ref[...] = run_s[...]
    route = jnp.where(lane == 0, i1.astype(F32), i2.astype(F32))
    for j, val in ((2, r1), (3, r2), (4, w1), (5, w2)):
        route = jnp.where(lane == j, val, route)
    route_ref[...] = jnp.where(lane >= 6, 0.0, route)


def _merge(o_a, o_b, proj, x, mod_l, gains_l, wba, wbb, wo, router, tm):
    bsz, t_len, d = x.shape
    with_router = router is not None
    tok = pl.BlockSpec((None, tm, d), lambda b, i: (b, i, 0))
    in_specs = [
        tok, tok,
        pl.BlockSpec((None, tm, 2 * d), lambda b, i: (b, i, OFF_MGA // (2 * d))),
        tok,
        pl.BlockSpec((N_MOD, None, 1, d), lambda b, i: (0, b, 0, 0)),
        pl.BlockSpec((4, d), lambda b, i: (0, 0)),
        _resident((d, d), lambda b, i: (0, 0)),
        _resident((d, d), lambda b, i: (0, 0)),
        _resident((d, d), lambda b, i: (0, 0)),
    ]
    args = [o_a, o_b, proj, x, mod_l, gains_l, wba, wbb, wo]
    if not with_router:
        return pl.pallas_call(
            functools.partial(_merge_kernel, with_router=False),
            grid=(bsz, t_len // tm),
            in_specs=in_specs,
            out_specs=[tok, tok],
            out_shape=[jax.ShapeDtypeStruct((bsz, t_len, d), F32),
                       jax.ShapeDtypeStruct((bsz, t_len, d), BF16)],
            compiler_params=_params(("parallel", "parallel")),
            name="merge",
        )(*args)
    w_r, b_r = router
    strict_lower = jnp.tril(jnp.ones((tm, tm), BF16), -1)
    in_specs += [pl.BlockSpec((d, N_EXPERTS), lambda b, i: (0, 0)),
                 pl.BlockSpec((1, N_EXPERTS), lambda b, i: (0, 0)),
                 pl.BlockSpec((tm, tm), lambda b, i: (0, 0))]
    args += [w_r, b_r.reshape(1, N_EXPERTS), strict_lower]
    return pl.pallas_call(
        functools.partial(_merge_kernel, with_router=True),
        grid=(bsz, t_len // tm),
        in_specs=in_specs,
        out_specs=[tok, tok,
                   pl.BlockSpec((None, tm, N_EXPERTS), lambda b, i: (b, i, 0)),
                   pl.BlockSpec((1, N_EXPERTS), lambda b, i: (0, 0))],
        out_shape=[jax.ShapeDtypeStruct((bsz, t_len, d), F32),
                   jax.ShapeDtypeStruct((bsz, t_len, d), F32),
                   jax.ShapeDtypeStruct((bsz, t_len, N_EXPERTS), F32),
                   jax.ShapeDtypeStruct((1, N_EXPERTS), F32)],
        scratch_shapes=[pltpu.VMEM((1, N_EXPERTS), F32)],
        compiler_params=_params(("arbitrary", "arbitrary")),
        name="merge_router",
    )(*args)


def _swiglu_hidden(hb, wg_ref, wu_ref, a_s):
    for c in range(FFN_DIM // MXU_N):
        sl = slice(c * MXU_N, (c + 1) * MXU_N)
        gate = _dot(hb, wg_ref[:, sl])
        up = _dot(hb, wu_ref[:, sl])
        a_s[:, sl] = (gate * _sigmoid(gate) * up).astype(BF16)


def _ffn_dense_kernel(h_ref, wg_ref, wu_ref, wd_ref, x_ref, mod_ref, gain_ref, o_ref, a_s):
    _swiglu_hidden(h_ref[...], wg_ref, wu_ref, a_s)
    y = _dot(a_s[...], wd_ref[...])
    o_ref[...] = x_ref[...] + mod_ref[0] * _rms(y, gain_ref[...])


def _ffn_dense(h2, wg, wu, wd, x, mod_l, gain, tm):
    bsz, t_len, d = x.shape
    tok = pl.BlockSpec((None, tm, d), lambda b, i: (b, i, 0))
    return pl.pallas_call(
        _ffn_dense_kernel,
        grid=(bsz, t_len // tm),
        in_specs=[
            tok,
            _resident((d, FFN_DIM), lambda b, i: (0, 0)),
            _resident((d, FFN_DIM), lambda b, i: (0, 0)),
            _resident((FFN_DIM, d), lambda b, i: (0, 0)),
            tok,
            pl.BlockSpec((1, None, 1, d), lambda b, i: (N_MOD - 1, b, 0, 0)),
            pl.BlockSpec((1, d), lambda b, i: (0, 0)),
        ],
        out_specs=tok,
        out_shape=jax.ShapeDtypeStruct((bsz, t_len, d), F32),
        scratch_shapes=[pltpu.VMEM((tm, FFN_DIM), BF16)],
        compiler_params=_params(("parallel", "parallel")),
        name="ffn_dense",
    )(h2, wg, wu, wd, x, mod_l, gain)


def _ffn_grouped_kernel(te_ref, nu_ref, x_ref, wg_ref, wu_ref, wd_ref, y_ref, a_s):
    del te_ref

    @pl.when(pl.program_id(0) < nu_ref[0])
    def _():
        tmg = x_ref.shape[0]
        x = x_ref[...].reshape(tmg, D_MODEL)
        _swiglu_hidden(x.astype(BF16), wg_ref, wu_ref, a_s)
        y_ref[...] = _dot(a_s[...], wd_ref[...]).reshape(y_ref.shape)


def _ffn_grouped(x_sorted, tile_expert, n_used, wg, wu, wd, tmg):
    n_slots = x_sorted.shape[0]
    d = D_MODEL
    n_tiles = n_slots // tmg

    def row_map(i, te, nu):
        return (jnp.minimum(i, nu[0] - 1), 0, 0)

    def w_map(i, te, nu):
        return (te[i], 0, 0)

    return pl.pallas_call(
        _ffn_grouped_kernel,
        grid_spec=pltpu.PrefetchScalarGridSpec(
            num_scalar_prefetch=2,
            grid=(n_tiles,),
            in_specs=[
                pl.BlockSpec((tmg,) + ROW_TILE, row_map),
                pl.BlockSpec((None, d, FFN_DIM), w_map),
                pl.BlockSpec((None, d, FFN_DIM), w_map),
                pl.BlockSpec((None, FFN_DIM, d), w_map),
            ],
            out_specs=pl.BlockSpec((tmg,) + ROW_TILE, row_map),
            scratch_shapes=[pltpu.VMEM((tmg, FFN_DIM), BF16)],
        ),
        out_shape=jax.ShapeDtypeStruct((n_slots,) + ROW_TILE, F32),
        compiler_params=_params(("arbitrary",)),
        name="ffn_grouped",
    )(tile_expert, n_used, x_sorted, wg, wu, wd)


def _row_copy(src_ref, dst_ref, sem):
    return pltpu.make_async_copy(src_ref, dst_ref, sem)


def _start_rows(copies, n_rows):
    def issue(r, carry):
        for i, cp in enumerate(copies(r)):
            cp.start(priority=i % 2)
        return carry

    lax.fori_loop(0, n_rows, issue, 0, unroll=8)


def _wait_rows(copies, n_rows):
    def drain(r, carry):
        for cp in copies(r):
            cp.wait()
        return carry

    lax.fori_loop(0, n_rows, drain, 0, unroll=8)


def _scatter_kernel(s1_ref, s2_ref, p1_ref, p2_ref, ztile_ref, h_ref, xs_ref,
                    rows_s, zero_s, sems, zsem):
    tm = h_ref.shape[0]
    tmg = zero_s.shape[0]
    i = pl.program_id(0)
    par = i % 2

    @pl.when(i == 0)
    def _():
        zero_s[...] = jnp.zeros_like(zero_s)

        def zero_copy(e):
            return pltpu.make_async_copy(zero_s, xs_ref.at[pl.ds(ztile_ref[e], tmg)], zsem)

        for e in range(N_EXPERTS):
            pl.when(ztile_ref[e] >= 0)(lambda e=e: zero_copy(e).start())
        for e in range(N_EXPERTS):
            pl.when(ztile_ref[e] >= 0)(lambda e=e: zero_copy(e).wait())

    def row_copies(slot_a, slot_b, p):
        def copies(r):
            src = rows_s.at[p, pl.ds(r, 1)]
            return (_row_copy(src, xs_ref.at[pl.ds(slot_a[0, r], 1)], sems.at[p]),
                    _row_copy(src, xs_ref.at[pl.ds(slot_b[0, r], 1)], sems.at[p]))
        return copies

    rows_s[par] = h_ref[...].reshape(rows_s.shape[1:])
    _start_rows(row_copies(s1_ref, s2_ref, par), tm)
    pl.when(i > 0)(lambda: _wait_rows(row_copies(p1_ref, p2_ref, 1 - par), tm))
    pl.when(i == pl.num_programs(0) - 1)(lambda: _wait_rows(row_copies(s1_ref, s2_ref, par), tm))


def _scatter_rows(h2, slot1, slot2, zero_tile_row, n_slots, tm, tmg):
    bsz, t_len, d = h2.shape
    n = bsz * t_len // tm

    def slot_spec(shift):
        return pl.BlockSpec((None, 1, tm), lambda i: (jnp.clip(i + shift, 0, n - 1), 0, 0),
                            memory_space=pltpu.SMEM)

    return pl.pallas_call(
        _scatter_kernel,
        grid=(n,),
        in_specs=[slot_spec(0), slot_spec(0), slot_spec(-1), slot_spec(-1),
                  pl.BlockSpec(memory_space=pltpu.SMEM),
                  pl.BlockSpec((tm, d), lambda i: (i, 0))],
        out_specs=pl.BlockSpec(memory_space=pl.ANY),
        out_shape=jax.ShapeDtypeStruct((n_slots,) + ROW_TILE, F32),
        scratch_shapes=[pltpu.VMEM((2, tm) + ROW_TILE, F32), pltpu.VMEM((tmg,) + ROW_TILE, F32),
                        pltpu.SemaphoreType.DMA((2,)), pltpu.SemaphoreType.DMA],
        compiler_params=_params(("arbitrary",)),
        name="scatter_rows",
    )(slot1, slot2, slot1, slot2, zero_tile_row, h2.reshape(bsz * t_len, d))


def _combine_kernel(s1_ref, s2_ref, n1_ref, n2_ref, route_ref, x_ref, mod_ref, gain_ref, y_ref,
                    o_ref, buf1, buf2, sems):
    tm, d = x_ref.shape
    i = pl.program_id(0)
    par = i % 2

    def row_copies(slot_a, slot_b, p):
        def copies(r):
            return (_row_copy(y_ref.at[pl.ds(slot_a[0, r], 1)], buf1.at[p, pl.ds(r, 1)], sems.at[p]),
                    _row_copy(y_ref.at[pl.ds(slot_b[0, r], 1)], buf2.at[p, pl.ds(r, 1)], sems.at[p]))
        return copies

    pl.when(i == 0)(lambda: _start_rows(row_copies(s1_ref, s2_ref, par), tm))
    pl.when(i + 1 < pl.num_programs(0))(
        lambda: _start_rows(row_copies(n1_ref, n2_ref, 1 - par), tm))
    _wait_rows(row_copies(s1_ref, s2_ref, par), tm)
    f_out = (route_ref[:, 4:5] * buf1[par].reshape(tm, d)
             + route_ref[:, 5:6] * buf2[par].reshape(tm, d))
    o_ref[...] = x_ref[...] + mod_ref[0] * _rms(f_out, gain_ref[...])


def _combine_rows(y_sorted, slot1, slot2, route, x, mod_l, gain, tm):
    bsz, t_len, d = x.shape
    nt = t_len // tm
    n = bsz * nt

    def slot_spec(shift):
        return pl.BlockSpec((None, 1, tm), lambda i: (jnp.minimum(i + shift, n - 1), 0, 0),
                            memory_space=pltpu.SMEM)

    tok = pl.BlockSpec((None, tm, d), lambda i: (i // nt, i % nt, 0))
    return pl.pallas_call(
        _combine_kernel,
        grid=(n,),
        in_specs=[slot_spec(0), slot_spec(0), slot_spec(1), slot_spec(1),
                  pl.BlockSpec((None, tm, N_EXPERTS), lambda i: (i // nt, i % nt, 0)),
                  tok,
                  pl.BlockSpec((1, None, 1, d), lambda i: (N_MOD - 1, i // nt, 0, 0)),
                  pl.BlockSpec((1, d), lambda i: (0, 0)),
                  pl.BlockSpec(memory_space=pl.ANY)],
        out_specs=tok,
        out_shape=jax.ShapeDtypeStruct((bsz, t_len, d), F32),
        scratch_shapes=[pltpu.VMEM((2, tm) + ROW_TILE, F32), pltpu.VMEM((2, tm) + ROW_TILE, F32),
                        pltpu.SemaphoreType.DMA((2,))],
        compiler_params=_params(("arbitrary",)),
        name="combine_rows",
    )(slot1, slot2, slot1, slot2, route, x, mod_l, gain, y_sorted)


def _moe_ffn(h2, route, counts, wg, wu, wd, x, mod_l, gain, tm, tmg):
    bsz, t_len, d = x.shape
    n_tok = bsz * t_len
    n_tiles = (2 * n_tok) // tmg + N_EXPERTS
    e1 = route[..., 0].astype(jnp.int32).reshape(n_tok)
    e2 = route[..., 1].astype(jnp.int32).reshape(n_tok)
    r1 = route[..., 2].astype(jnp.int32).reshape(n_tok)
    r2 = route[..., 3].astype(jnp.int32).reshape(n_tok)
    cnt = counts[0].astype(jnp.int32)
    tiles_per = (cnt + tmg - 1) // tmg
    tile_end = jnp.cumsum(tiles_per)
    offs = (tile_end - tiles_per) * tmg
    n_used = tile_end[-1:]
    tile_ids = jnp.arange(n_tiles, dtype=jnp.int32)
    tile_expert = jnp.sum(tile_ids[:, None] >= tile_end[None, :], axis=1).astype(jnp.int32)
    last_expert = jnp.sum(n_used - 1 >= tile_end).astype(jnp.int32)
    tile_expert = jnp.where(tile_ids < n_used, tile_expert, last_expert)
    zero_tile_row = jnp.where(cnt > 0, (tile_end - 1) * tmg, -1).astype(jnp.int32)
    nt = t_len // tm
    slot1 = (offs[e1] + r1).reshape(bsz * nt, 1, tm)
    slot2 = (offs[e2] + r2).reshape(bsz * nt, 1, tm)
    x_sorted = _scatter_rows(h2, slot1, slot2, zero_tile_row, n_tiles * tmg, tm, tmg)
    y_sorted = _ffn_grouped(x_sorted, tile_expert, n_used, wg, wu, wd, tmg)
    return _combine_rows(y_sorted, slot1, slot2, route, x, mod_l, gain, tm)


def _rope_tables(pos):
    half = DK_B // 2
    inv = ROPE_BASE ** (-jnp.arange(half, dtype=F32) / half)
    ang = pos.astype(F32)[:, None] * inv[None, :]
    cos, sin = jnp.cos(ang), jnp.sin(ang)
    return jnp.concatenate([cos, cos], axis=-1), jnp.concatenate([-sin, sin], axis=-1)


def _trunk(x, mod, pos, s_a_in, s_b_in, lower_bounds, wts, tm, tb, tmg):
    depth = mod.shape[0]
    bsz = x.shape[0]
    cos_t, sin_t = _rope_tables(pos)
    sa_all = jnp.zeros((depth, bsz, H_A, DK_A, DV_A), F32)
    sb_all = jnp.zeros((depth, bsz, H_B, DK_B, DV_B), F32)
    for l in range(depth):
        mod_l = mod[l]
        gains_l = wts["norm_gains"][l]
        if s_a_in is None:
            s_a = jnp.zeros((bsz, H_A, DK_A, DV_A), F32)
            s_b = jnp.zeros((bsz, H_B, DK_B, DV_B), F32)
        else:
            s_a, s_b = s_a_in[l], s_b_in[l]
        proj, g_hi, g_lo = _in_projection(x, mod_l, gains_l[0:1], lower_bounds[l:l + 1],
                                          cos_t, sin_t, wts["w_in"][l], tm)
        o_a, o_b, sa_all, sb_all = _recurrences(
            proj, g_hi, g_lo,
            wts["g_norm_a"][l].reshape(H_A, 1, DV_A), wts["g_norm_b"][l].reshape(H_B, 1, DV_B),
            s_a, s_b, sa_all, sb_all, l, tb)
        j = l // 2
        router = None if l % 2 == 0 else (wts["w_router"][j], wts["b_router"][j])
        outs = _merge(o_a, o_b, proj, x, mod_l, gains_l, wts["w_branch_a"][l],
                      wts["w_branch_b"][l], wts["w_out"][l], router, tm)
        if l % 2 == 0:
            x, h2 = outs
            x = _ffn_dense(h2, wts["w_gate_dense"][j], wts["w_up_dense"][j],
                           wts["w_down_dense"][j], x, mod_l, gains_l[3:4], tm)
        else:
            x, h2, route, counts = outs
            x = _moe_ffn(h2, route, counts, wts["w_gate_exp"][j], wts["w_up_exp"][j],
                         wts["w_down_exp"][j], x, mod_l, gains_l[3:4], tm, tmg)
    return x, sa_all, sb_all


def kernel(x_prompt, x_sample, state_hgrn, state_ret, c_prompt, c_sample, w_mod, b_mod, norm_gains,
           w_in, lb_logits, g_norm_a, g_norm_b, w_branch_a, w_branch_b, w_out, w_gate_dense,
           w_up_dense, w_down_dense, w_router, b_router, w_gate_exp, w_up_exp, w_down_exp):
    n_prompt = x_prompt.shape[0]
    past_len = 1024
    wts = dict(
        norm_gains=norm_gains, g_norm_a=g_norm_a, g_norm_b=g_norm_b,
        w_router=w_router, b_router=b_router,
        w_in=w_in.astype(BF16), w_branch_a=w_branch_a.astype(BF16),
        w_branch_b=w_branch_b.astype(BF16), w_out=w_out.astype(BF16),
        w_gate_dense=w_gate_dense.astype(BF16), w_up_dense=w_up_dense.astype(BF16),
        w_down_dense=w_down_dense.astype(BF16), w_gate_exp=w_gate_exp.astype(BF16),
        w_up_exp=w_up_exp.astype(BF16), w_down_exp=w_down_exp.astype(BF16))
    lb_p = jax.nn.softmax(lb_logits.astype(F32), axis=0)
    lower_bounds = jnp.clip(jnp.cumsum(lb_p, axis=0) - lb_p[:1], 0.0, 1.0)
    c_all = jnp.concatenate([c_prompt, c_sample], axis=0)
    mod = _modulation(c_all, w_mod, b_mod)
    mod = mod.reshape(mod.shape[:3] + (1, D_MODEL))
    pos_prompt = jnp.arange(x_prompt.shape[1], dtype=jnp.int32)
    pos_sample = past_len + jnp.arange(x_sample.shape[1], dtype=jnp.int32)
    t_p, t_s = x_prompt.shape[1], x_sample.shape[1]
    y_p, hg_p, rt_p = _trunk(x_prompt, mod[:, :, :n_prompt], pos_prompt, None, None,
                             lower_bounds, wts, tm=min(512, t_p), tb=min(1024, t_p), tmg=512)
    y_s, hg_s, rt_s = _trunk(x_sample, mod[:, :, n_prompt:], pos_sample, state_hgrn, state_ret,
                             lower_bounds, wts, tm=min(512, t_s), tb=min(512, t_s), tmg=64)
    return (y_p, y_s, hg_p, rt_p, hg_s, rt_s)
```

```python
import functools
import math

import jax
import jax.numpy as jnp
from jax import lax
from jax.experimental import pallas as pl
from jax.experimental.pallas import tpu as pltpu

F32 = jnp.float32
BF16 = jnp.bfloat16

D_MODEL = 1024
H_A, DK_A, DV_A = 4, 128, 256
H_B, DK_B, DV_B = 4, 128, 256
ROPE_BASE = 10000.0
FFN_DIM = 2816
N_EXPERTS = 8
RMS_EPS = 1e-6
F_FLOOR = 1e-20
N_MOD = 6
N_IN = 8 * D_MODEL
OFF_QA, OFF_KA, OFF_IA, OFF_OGA = 0, 512, 1024, 2048
OFF_QB, OFF_KB, OFF_VB, OFF_SGB = 3072, 3584, 4096, 5120
OFF_MGA, OFF_MGB = 6144, 7168
REC_COLS = 6144
SUB = 8
LOG2_GAMMA = tuple(math.log2(1.0 - 2.0 ** (-5.0 - h)) for h in range(H_B))
VMEM_LIMIT = 56 * 1024 * 1024
MXU_N = 256
ROW_TILE = (8, 128)


def _params(sem):
    return pltpu.CompilerParams(dimension_semantics=sem, vmem_limit_bytes=VMEM_LIMIT)


def _resident(shape, index_map):
    return pl.BlockSpec(shape, index_map, pipeline_mode=pl.Buffered(1))


def _rms(x, gain):
    return x * lax.rsqrt(jnp.mean(x * x, axis=-1, keepdims=True) + RMS_EPS) * gain


def _unit_rms(x):
    return x * lax.rsqrt(jnp.mean(x * x, axis=-1, keepdims=True) + RMS_EPS)


def _sigmoid(x):
    return 1.0 / (1.0 + jnp.exp(-x))


def _dot(a, b):
    return jnp.dot(a, b, preferred_element_type=F32)


def _dot_nt(a, b):
    return lax.dot_general(a, b, (((1,), (1,)), ((), ())), preferred_element_type=F32)


def _dot_tn(a, b):
    return lax.dot_general(a, b, (((0,), (0,)), ((), ())), preferred_element_type=F32)


def _split_bf16(x):
    hi = x.astype(BF16)
    return hi, (x - hi.astype(F32)).astype(BF16)


def _mod_kernel(c_ref, w_ref, b_ref, o_ref):
    c = c_ref[...]
    c_act = (c * _sigmoid(c)).astype(BF16)
    o_ref[...] = _dot(c_act, w_ref[...].astype(BF16)) + b_ref[...]


def _modulation(c_all, w_mod, b_mod):
    depth = w_mod.shape[0]
    n_seq = c_all.shape[0]
    d = D_MODEL
    return pl.pallas_call(
        _mod_kernel,
        grid=(depth, N_MOD),
        in_specs=[
            pl.BlockSpec((n_seq, d), lambda l, j: (0, 0)),
            pl.BlockSpec((None, d, d), lambda l, j: (l, 0, j)),
            pl.BlockSpec((None, None, 1, d), lambda l, j: (l, j, 0, 0)),
        ],
        out_specs=pl.BlockSpec((None, None, n_seq, d), lambda l, j: (l, j, 0, 0)),
        out_shape=jax.ShapeDtypeStruct((depth, N_MOD, n_seq, d), F32),
        compiler_params=_params(("parallel", "parallel")),
        name="modulation",
    )(c_all, w_mod, b_mod.reshape(depth, N_MOD, 1, d))


def _inproj_kernel(x_ref, mod_ref, gain_ref, lb_ref, gna_ref, gnb_ref, cos_ref, sin_ref, w_ref,
                   o_ref, ghi_ref, glo_ref):
    d = D_MODEL
    h = _rms(x_ref[...], gain_ref[...]) * (1.0 + mod_ref[1]) + mod_ref[0]
    hb = h.astype(BF16)

    def seg(cols):
        return _dot(hb, w_ref[:, cols[0]:cols[0] + cols[1]])

    def hgrn_q(q):
        o_ref[:, OFF_QA:OFF_QA + 512] = (q * _sigmoid(q)).astype(BF16)

    def hgrn_f(f_raw):
        lb = lb_ref[...]
        sig = _sigmoid(f_raw)
        g2 = jnp.log2(jnp.maximum(lb + (1.0 - lb) * sig, F_FLOOR))
        ghi, glo = _split_bf16(g2)
        ghi_ref[...] = ghi
        glo_ref[...] = glo
        o_ref[:, OFF_KA:OFF_KA + 512] = ((1.0 - lb) * (1.0 - sig)).astype(BF16)

    def retention_qk(y):
        cos, sin = cos_ref[...], sin_ref[...]
        for hh in range(H_B):
            xq = y[:, hh * DK_B:(hh + 1) * DK_B]
            xk = y[:, 512 + hh * DK_B:512 + (hh + 1) * DK_B]
            rq = xq * cos + pltpu.roll(xq, DK_B // 2, axis=1) * sin
            rk = (xk * cos + pltpu.roll(xk, DK_B // 2, axis=1) * sin) * (DK_B ** -0.5)
            o_ref[:, OFF_QB + hh * DK_B:OFF_QB + (hh + 1) * DK_B] = rq.astype(BF16)
            o_ref[:, OFF_KB + hh * DK_B:OFF_KB + (hh + 1) * DK_B] = rk.astype(BF16)

    def store(off, act):
        def epilogue(y):
            o_ref[:, off:off + d] = act(y).astype(BF16)
        return epilogue

    plan = [((1 * d, d), store(OFF_IA, lambda y: y)), ((512, 512), hgrn_f),
            ((2 * d, d), store(OFF_OGA, lambda y: _sigmoid(y) * gna_ref[...])), ((0, 512), hgrn_q),
            ((3 * d, d), retention_qk),
            ((5 * d, d), store(OFF_SGB, lambda y: y * _sigmoid(y) * gnb_ref[...])),
            ((6 * d, d), store(OFF_MGA, _sigmoid)), ((7 * d, d), store(OFF_MGB, _sigmoid)),
            ((4 * d, d), store(OFF_VB, lambda y: y))]
    y = seg(plan[0][0])
    for i, (_, epilogue) in enumerate(plan):
        y_next = seg(plan[i + 1][0]) if i + 1 < len(plan) else None
        epilogue(y)
        y = y_next


def _in_projection(x, mod_l, gain, lb, gna, gnb, cos_t, sin_t, w_in_l, tm):
    bsz, t_len, d = x.shape
    g_spec = pl.BlockSpec((None, tm, H_A * DK_A), lambda b, i: (b, i, 0))
    g_shape = jax.ShapeDtypeStruct((bsz, t_len, H_A * DK_A), BF16)
    return pl.pallas_call(
        _inproj_kernel,
        grid=(bsz, t_len // tm),
        in_specs=[
            pl.BlockSpec((None, tm, d), lambda b, i: (b, i, 0)),
            pl.BlockSpec((3, None, 1, d), lambda b, i: (0, b, 0, 0)),
            pl.BlockSpec((1, d), lambda b, i: (0, 0)),
            pl.BlockSpec((1, H_A * DK_A), lambda b, i: (0, 0)),
            pl.BlockSpec((1, d), lambda b, i: (0, 0)),
            pl.BlockSpec((1, d), lambda b, i: (0, 0)),
            pl.BlockSpec((tm, DK_B), lambda b, i: (i, 0)),
            pl.BlockSpec((tm, DK_B), lambda b, i: (i, 0)),
            _resident((d, N_IN), lambda b, i: (0, 0)),
        ],
        out_specs=[pl.BlockSpec((None, tm, N_IN), lambda b, i: (b, i, 0)), g_spec, g_spec],
        out_shape=[jax.ShapeDtypeStruct((bsz, t_len, N_IN), BF16), g_shape, g_shape],
        compiler_params=_params(("parallel", "parallel")),
        name="in_projection",
    )(x, mod_l, gain, lb, gna, gnb, cos_t, sin_t, w_in_l)


def _hgrn_diag_lhs(q, k, gc, x_s, row0, l_a):
    pack = 2 * SUB
    for m in range(l_a // pack):
        pieces = []
        for s in range(SUB):
            halves = []
            for lo in (m * pack, m * pack + SUB):
                decay = jnp.exp2(jnp.minimum(gc[lo:lo + SUB] - gc[lo + s:lo + s + 1], 0.0))
                halves.append(q[lo:lo + SUB] * decay * k[lo + s:lo + s + 1])
            pieces.append(jnp.concatenate(halves, axis=0).astype(BF16))
        x_s[row0 + m * pack:row0 + (m + 1) * pack, :] = jnp.concatenate(pieces, axis=1)


def _hgrn_off_diag(q, k, gc, l_a):
    q_parts, k_parts = [], []
    for j in range(l_a // SUB - 1):
        lo, hi = j * SUB, (j + 1) * SUB
        g_ref = gc[hi - 1:hi]
        q_hat = q[hi:] * jnp.exp2(gc[hi:] - g_ref)
        k_bar = k[lo:hi] * jnp.exp2(g_ref - gc[lo:hi])
        q_parts.append(jnp.concatenate([jnp.zeros((hi, DK_A), F32), q_hat], axis=0))
        k_rows = [k_bar, jnp.zeros((l_a - hi, DK_A), F32)]
        if lo:
            k_rows.insert(0, jnp.zeros((lo, DK_A), F32))
        k_parts.append(jnp.concatenate(k_rows, axis=0))
    return _dot_nt(jnp.concatenate(q_parts, axis=1).astype(BF16),
                   jnp.concatenate(k_parts, axis=1).astype(BF16))


def _recurrence_kernel(proj_ref, ghi_ref, glo_ref, bsel_ref, sa0_ref, sb0_ref,
                       sa_all_ref, sb_all_ref, oa_ref, ob_ref, sa_out_ref, sb_out_ref,
                       sa_s, sb_s, x_s, a_sc, qt_sc, pk_sc, *, tb, l_a, l_b):
    del sa_all_ref, sb_all_ref
    t = pl.program_id(1)

    @pl.when(t == 0)
    def _():
        sa_s[...] = sa0_ref[...]
        sb_s[...] = sb0_ref[...]

    ri = lax.broadcasted_iota(jnp.int32, (l_a, l_a), 0)
    ci = lax.broadcasted_iota(jnp.int32, (l_a, l_a), 1)
    tril = (ri >= ci).astype(BF16)
    diag_mask = ((ri // SUB) == (ci // SUB)) & (ri >= ci)

    def chunk_rows(c):
        return pl.ds(pl.multiple_of(c * l_a, l_a), l_a)

    def stage1_prep(c):
        rows = chunk_rows(c)
        gc_all = _dot(tril, ghi_ref[rows, :]) + _dot(tril, glo_ref[rows, :])
        q_all = proj_ref[rows, OFF_QA:OFF_QA + H_A * DK_A].astype(F32)
        k_all = proj_ref[rows, OFF_KA:OFF_KA + H_A * DK_A].astype(F32)
        return gc_all, q_all, k_all

    def stage1_head(prep, h, slot):
        cols = slice(h * DK_A, (h + 1) * DK_A)
        gc, q, k = (a[:, cols] for a in prep)
        _hgrn_diag_lhs(q, k, gc, x_s, h * l_a, l_a)
        a = _dot(x_s[h * l_a:(h + 1) * l_a, :], bsel_ref[...])
        off = _hgrn_off_diag(q, k, gc, l_a) if l_a > SUB else 0.0
        a_sc[slot, h] = jnp.where(diag_mask, a, off).astype(BF16)
        qt_sc[slot, h] = (q * jnp.exp2(gc)).astype(BF16)
        g_end = gc[l_a - 1:l_a]
        k_tail = k * jnp.exp2(g_end - gc)
        packed = jnp.concatenate(
            [k_tail, jnp.broadcast_to(jnp.exp2(g_end), (DK_A - l_a, DK_A))], axis=0)
        pk_sc[slot, h] = packed.T

    def stage2_head(c, h, slot):
        rows = chunk_rows(c)
        v = proj_ref[rows, OFF_IA + h * DV_A:OFF_IA + (h + 1) * DV_A]
        og = proj_ref[rows, OFF_OGA + h * DV_A:OFF_OGA + (h + 1) * DV_A]
        s_prev = sa_s[h]
        o = _dot(a_sc[slot, h], v) + _dot(qt_sc[slot, h], s_prev.astype(BF16))
        packed_t = pk_sc[slot, h]
        sa_s[h] = (s_prev * packed_t[:, l_a:l_a + 1]
                   + _dot(packed_t[:, :l_a].astype(BF16), v))
        oa_ref[rows, h * DV_A:(h + 1) * DV_A] = _unit_rms(o).astype(BF16) * og

    ti = lax.broadcasted_iota(jnp.int32, (l_b, l_b), 0)
    si = lax.broadcasted_iota(jnp.int32, (l_b, l_b), 1)
    rel = (ti - si).astype(F32)
    tcol = lax.broadcasted_iota(jnp.int32, (l_b, 1), 0).astype(F32)
    decays = [jnp.where(ti >= si, jnp.exp2(jnp.minimum(rel * lg, 0.0)), 0.0) for lg in LOG2_GAMMA]
    q_scale = [jnp.exp2((tcol + 1.0) * lg) for lg in LOG2_GAMMA]
    k_scale = [jnp.exp2((l_b - 1.0 - tcol) * lg) for lg in LOG2_GAMMA]

    def retention_head(c, h):
        rows = pl.ds(pl.multiple_of(c * l_b, l_b), l_b)
        qb = proj_ref[rows, OFF_QB + h * DK_B:OFF_QB + (h + 1) * DK_B]
        kb = proj_ref[rows, OFF_KB + h * DK_B:OFF_KB + (h + 1) * DK_B]
        v = proj_ref[rows, OFF_VB + h * DV_B:OFF_VB + (h + 1) * DV_B]
        sg = proj_ref[rows, OFF_SGB + h * DV_B:OFF_SGB + (h + 1) * DV_B]
        scores = _dot_nt(qb, kb) * decays[h]
        s_prev = sb_s[h]
        o = (_dot(scores.astype(BF16), v)
             + _dot((qb.astype(F32) * q_scale[h]).astype(BF16), s_prev.astype(BF16)))
        k_tail = (kb.astype(F32) * k_scale[h]).astype(BF16)
        sb_s[h] = s_prev * (2.0 ** (l_b * LOG2_GAMMA[h])) + _dot_tn(k_tail, v)
        ob_ref[rows, h * DV_B:(h + 1) * DV_B] = _unit_rms(o).astype(BF16) * sg

    n_a = tb // l_a
    prep = stage1_prep(0)
    for h in range(H_A):
        stage1_head(prep, h, 0)
    if n_a == 1:
        for h in range(H_A):
            stage2_head(0, h, 0)
            retention_head(0, h)
    else:
        def step(s, carry):
            prep = stage1_prep(2 * s + 1)
            for h in range(H_A):
                stage1_head(prep, h, 1)
                stage2_head(2 * s, h, 0)
            prep = stage1_prep(jnp.minimum(2 * s + 2, n_a - 1))
            for h in range(H_A):
                stage1_head(prep, h, 0)
                stage2_head(2 * s + 1, h, 1)
                retention_head(s, h)
            return carry

        lax.fori_loop(0, n_a // 2, step, 0)

    @pl.when(t == pl.num_programs(1) - 1)
    def _():
        sa_out_ref[...] = sa_s[...]
        sb_out_ref[...] = sb_s[...]


def _diag_selector(l_a):
    s_of_row = jnp.arange(SUB * DK_A, dtype=jnp.int32) // DK_A
    col = jnp.arange(l_a, dtype=jnp.int32)
    return (s_of_row[:, None] == (col[None, :] % SUB)).astype(BF16)


def _recurrences(proj, g_hi, g_lo, s_a0, s_b0, sa_all, sb_all, layer, tb):
    bsz, t_len, _ = proj.shape
    l_a = min(64, t_len)
    l_b = min(128, t_len)
    state_spec = pl.BlockSpec((None, H_A, DK_A, DV_A), lambda b, t: (b, 0, 0, 0))
    state_out_spec = pl.BlockSpec((None, None, H_A, DK_A, DV_A), lambda b, t: (layer, b, 0, 0, 0))
    hbm = pl.BlockSpec(memory_space=pl.ANY)
    act_spec = pl.BlockSpec((None, tb, D_MODEL), lambda b, t: (b, t, 0))
    g_spec = pl.BlockSpec((None, tb, H_A * DK_A), lambda b, t: (b, t, 0))
    return pl.pallas_call(
        functools.partial(_recurrence_kernel, tb=tb, l_a=l_a, l_b=l_b),
        grid=(bsz, t_len // tb),
        in_specs=[
            pl.BlockSpec((None, tb, REC_COLS), lambda b, t: (b, t, 0)),
            g_spec, g_spec,
            pl.BlockSpec((SUB * DK_A, l_a), lambda b, t: (0, 0)),
            state_spec, state_spec, hbm, hbm,
        ],
        out_specs=[act_spec, act_spec, state_out_spec, state_out_spec],
        out_shape=[
            jax.ShapeDtypeStruct((bsz, t_len, D_MODEL), BF16),
            jax.ShapeDtypeStruct((bsz, t_len, D_MODEL), BF16),
            jax.ShapeDtypeStruct(sa_all.shape, F32),
            jax.ShapeDtypeStruct(sb_all.shape, F32),
        ],
        input_output_aliases={6: 2, 7: 3},
        scratch_shapes=[pltpu.VMEM((H_A, DK_A, DV_A), F32), pltpu.VMEM((H_B, DK_B, DV_B), F32),
                        pltpu.VMEM((H_A * l_a, SUB * DK_A), BF16),
                        pltpu.VMEM((2, H_A, l_a, l_a), BF16),
                        pltpu.VMEM((2, H_A, l_a, DK_A), BF16),
                        pltpu.VMEM((2, H_A, DK_A, DK_A), F32)],
        compiler_params=_params(("parallel", "arbitrary")),
        name="recurrences",
    )(proj, g_hi, g_lo, _diag_selector(l_a), s_a0, s_b0, sa_all, sb_all)


def _merge_kernel(*refs, with_router):
    if with_router:
        (oa_ref, ob_ref, mg_ref, x_ref, mod_ref, gains_ref, wba_ref, wbb_ref, wo_ref,
         wr_ref, br_ref, lt_ref, xo_ref, h2_ref, route_ref, cnt_ref, run_s) = refs
    else:
        (oa_ref, ob_ref, mg_ref, x_ref, mod_ref, gains_ref, wba_ref, wbb_ref, wo_ref,
         xo_ref, h2_ref) = refs
    d = D_MODEL
    tm = x_ref.shape[0]
    halves = [pl.ds(0, tm // 2), pl.ds(tm // 2, tm // 2)] if tm >= 256 else [pl.ds(0, tm)]
    merged = []
    for rows in halves:
        y_a = _dot(oa_ref[rows, :], wba_ref[...])
        y_b = _dot(ob_ref[rows, :], wbb_ref[...])
        merged.append((mg_ref[rows, :d].astype(F32) * y_a
                       + mg_ref[rows, d:].astype(F32) * y_b).astype(BF16))
    m_outs = [_dot(m, wo_ref[...]) for m in merged]
    if with_router:
        w_hi, w_lo = _split_bf16(wr_ref[...])
    logit_parts = []
    for rows, m_out in zip(halves, m_outs):
        x = x_ref[rows, :] + mod_ref[2] * _rms(m_out, gains_ref[1:2])
        xo_ref[rows, :] = x
        h2 = _rms(x, gains_ref[2:3]) * (1.0 + mod_ref[4]) + mod_ref[3]
        if not with_router:
            h2_ref[rows, :] = h2.astype(BF16)
            continue
        h2_ref[rows, :] = h2
        h_hi, h_lo = _split_bf16(h2)
        logit_parts.append(_dot_nt(w_hi, h_hi) + _dot_nt(w_lo, h_hi) + _dot_nt(w_hi, h_lo))
    if not with_router:
        return

    @pl.when((pl.program_id(0) == 0) & (pl.program_id(1) == 0))
    def _():
        run_s[...] = jnp.zeros_like(run_s)

    logits = jnp.concatenate(logit_parts, axis=1) + br_ref[...]
    exp_id = lax.broadcasted_iota(jnp.int32, logits.shape, 0)
    m1 = jnp.max(logits, axis=0, keepdims=True)
    i1 = jnp.min(jnp.where(logits == m1, exp_id, N_EXPERTS), axis=0, keepdims=True)
    rest = jnp.where(exp_id == i1, -jnp.inf, logits)
    m2 = jnp.max(rest, axis=0, keepdims=True)
    i2 = jnp.min(jnp.where(rest == m2, exp_id, N_EXPERTS), axis=0, keepdims=True)
    e2 = jnp.exp(m2 - m1)
    w1 = 1.0 / (1.0 + e2)
    w2 = e2 / (1.0 + e2)
    chosen = ((exp_id == i1) | (exp_id == i2)).astype(F32)
    before = run_s[...] + _dot(chosen.astype(BF16), lt_ref[...])
    r1 = jnp.sum(jnp.where(exp_id == i1, before, 0.0), axis=0, keepdims=True)
    r2 = jnp.sum(jnp.where(exp_id == i2, before, 0.0), axis=0, keepdims=True)
    run_s[...] = run_s[...] + jnp.sum(chosen, axis=1, keepdims=True)
    cnt_ref[...] = run_s[...]
    route = jnp.zeros(logits.shape, F32)
    for j, val in enumerate((i1.astype(F32), i2.astype(F32), r1, r2, w1, w2)):
        route = jnp.where(exp_id == j, val, route)
    route_ref[...] = route


def _merge(o_a, o_b, proj, x, mod_l, gains_l, wba, wbb, wo, router, tm):
    bsz, t_len, d = x.shape
    with_router = router is not None
    tok = pl.BlockSpec((None, tm, d), lambda b, i: (b, i, 0))
    in_specs = [
        tok, tok,
        pl.BlockSpec((None, tm, 2 * d), lambda b, i: (b, i, OFF_MGA // (2 * d))),
        tok,
        pl.BlockSpec((N_MOD, None, 1, d), lambda b, i: (0, b, 0, 0)),
        pl.BlockSpec((4, d), lambda b, i: (0, 0)),
        _resident((d, d), lambda b, i: (0, 0)),
        _resident((d, d), lambda b, i: (0, 0)),
        _resident((d, d), lambda b, i: (0, 0)),
    ]
    args = [o_a, o_b, proj, x, mod_l, gains_l, wba, wbb, wo]
    if not with_router:
        return pl.pallas_call(
            functools.partial(_merge_kernel, with_router=False),
            grid=(bsz, t_len // tm),
            in_specs=in_specs,
            out_specs=[tok, tok],
            out_shape=[jax.ShapeDtypeStruct((bsz, t_len, d), F32),
                       jax.ShapeDtypeStruct((bsz, t_len, d), BF16)],
            compiler_params=_params(("parallel", "parallel")),
            name="merge",
        )(*args)
    w_r, b_r = router
    earlier = jnp.triu(jnp.ones((tm, tm), BF16), 1)
    in_specs += [pl.BlockSpec((N_EXPERTS, d), lambda b, i: (0, 0)),
                 pl.BlockSpec((N_EXPERTS, 1), lambda b, i: (0, 0)),
                 pl.BlockSpec((tm, tm), lambda b, i: (0, 0))]
    args += [w_r.T, b_r.reshape(N_EXPERTS, 1), earlier]
    nt = t_len // tm
    return pl.pallas_call(
        functools.partial(_merge_kernel, with_router=True),
        grid=(bsz, nt),
        in_specs=in_specs,
        out_specs=[tok, tok,
                   pl.BlockSpec((None, None, N_EXPERTS, tm), lambda b, i: (b, i, 0, 0)),
                   pl.BlockSpec((N_EXPERTS, 1), lambda b, i: (0, 0))],
        out_shape=[jax.ShapeDtypeStruct((bsz, t_len, d), F32),
                   jax.ShapeDtypeStruct((bsz, t_len, d), F32),
                   jax.ShapeDtypeStruct((bsz, nt, N_EXPERTS, tm), F32),
                   jax.ShapeDtypeStruct((N_EXPERTS, 1), F32)],
        scratch_shapes=[pltpu.VMEM((N_EXPERTS, 1), F32)],
        compiler_params=_params(("arbitrary", "arbitrary")),
        name="merge_router",
    )(*args)


def _swiglu_hidden(hb, wg_ref, wu_ref, a_s):
    for c in range(FFN_DIM // MXU_N):
        sl = slice(c * MXU_N, (c + 1) * MXU_N)
        gate = _dot(hb, wg_ref[:, sl])
        up = _dot(hb, wu_ref[:, sl])
        a_s[:, sl] = (gate * _sigmoid(gate) * up).astype(BF16)


def _ffn_dense_kernel(h_ref, wg_ref, wu_ref, wd_ref, x_ref, mod_ref, gain_ref, o_ref, a_s):
    _swiglu_hidden(h_ref[...], wg_ref, wu_ref, a_s)
    y = _dot(a_s[...], wd_ref[...])
    o_ref[...] = x_ref[...] + mod_ref[0] * _rms(y, gain_ref[...])


def _ffn_dense(h2, wg, wu, wd, x, mod_l, gain, tm):
    bsz, t_len, d = x.shape
    tok = pl.BlockSpec((None, tm, d), lambda b, i: (b, i, 0))
    return pl.pallas_call(
        _ffn_dense_kernel,
        grid=(bsz, t_len // tm),
        in_specs=[
            tok,
            _resident((d, FFN_DIM), lambda b, i: (0, 0)),
            _resident((d, FFN_DIM), lambda b, i: (0, 0)),
            _resident((FFN_DIM, d), lambda b, i: (0, 0)),
            tok,
            pl.BlockSpec((1, None, 1, d), lambda b, i: (N_MOD - 1, b, 0, 0)),
            pl.BlockSpec((1, d), lambda b, i: (0, 0)),
        ],
        out_specs=tok,
        out_shape=jax.ShapeDtypeStruct((bsz, t_len, d), F32),
        scratch_shapes=[pltpu.VMEM((tm, FFN_DIM), BF16)],
        compiler_params=_params(("parallel", "parallel")),
        name="ffn_dense",
    )(h2, wg, wu, wd, x, mod_l, gain)


def _ffn_grouped_kernel(te_ref, nu_ref, x_ref, wg_ref, wu_ref, wd_ref, y_ref, a_s):
    del te_ref

    @pl.when(pl.program_id(0) < nu_ref[0])
    def _():
        tmg = x_ref.shape[0]
        x = x_ref[...].reshape(tmg, D_MODEL)
        _swiglu_hidden(x.astype(BF16), wg_ref, wu_ref, a_s)
        y_ref[...] = _dot(a_s[...], wd_ref[...]).reshape(y_ref.shape)


def _ffn_grouped(x_sorted, tile_expert, n_used, wg, wu, wd, tmg):
    n_slots = x_sorted.shape[0]
    d = D_MODEL
    n_tiles = n_slots // tmg

    def row_map(i, te, nu):
        return (jnp.minimum(i, nu[0] - 1), 0, 0)

    def w_map(i, te, nu):
        return (te[i], 0, 0)

    return pl.pallas_call(
        _ffn_grouped_kernel,
        grid_spec=pltpu.PrefetchScalarGridSpec(
            num_scalar_prefetch=2,
            grid=(n_tiles,),
            in_specs=[
                pl.BlockSpec((tmg,) + ROW_TILE, row_map),
                pl.BlockSpec((None, d, FFN_DIM), w_map),
                pl.BlockSpec((None, d, FFN_DIM), w_map),
                pl.BlockSpec((None, FFN_DIM, d), w_map),
            ],
            out_specs=pl.BlockSpec((tmg,) + ROW_TILE, row_map),
            scratch_shapes=[pltpu.VMEM((tmg, FFN_DIM), BF16)],
        ),
        out_shape=jax.ShapeDtypeStruct((n_slots,) + ROW_TILE, F32),
        compiler_params=_params(("arbitrary",)),
        name="ffn_grouped",
    )(tile_expert, n_used, x_sorted, wg, wu, wd)


def _row_copy(src_ref, dst_ref, sem):
    return pltpu.make_async_copy(src_ref, dst_ref, sem)


def _start_rows(copies, n_rows):
    def issue(r, carry):
        for i, cp in enumerate(copies(r)):
            cp.start(priority=i % 2)
        return carry

    lax.fori_loop(0, n_rows, issue, 0, unroll=8)


def _wait_rows(copies, n_rows):
    def drain(r, carry):
        for cp in copies(r):
            cp.wait()
        return carry

    lax.fori_loop(0, n_rows, drain, 0, unroll=8)


def _scatter_kernel(s1_ref, s2_ref, p1_ref, p2_ref, ztile_ref, h_ref, xs_ref,
                    rows_s, zero_s, sems, zsem):
    tm = h_ref.shape[0]
    tmg = zero_s.shape[0]
    i = pl.program_id(0)
    par = i % 2

    @pl.when(i == 0)
    def _():
        zero_s[...] = jnp.zeros_like(zero_s)

        def zero_copy(e):
            return pltpu.make_async_copy(zero_s, xs_ref.at[pl.ds(ztile_ref[e], tmg)], zsem)

        for e in range(N_EXPERTS):
            pl.when(ztile_ref[e] >= 0)(lambda e=e: zero_copy(e).start())
        for e in range(N_EXPERTS):
            pl.when(ztile_ref[e] >= 0)(lambda e=e: zero_copy(e).wait())

    def row_copies(slot_a, slot_b, p):
        def copies(r):
            src = rows_s.at[p, pl.ds(r, 1)]
            return (_row_copy(src, xs_ref.at[pl.ds(slot_a[0, r], 1)], sems.at[p]),
                    _row_copy(src, xs_ref.at[pl.ds(slot_b[0, r], 1)], sems.at[p]))
        return copies

    rows_s[par] = h_ref[...].reshape(rows_s.shape[1:])
    _start_rows(row_copies(s1_ref, s2_ref, par), tm)
    pl.when(i > 0)(lambda: _wait_rows(row_copies(p1_ref, p2_ref, 1 - par), tm))
    pl.when(i == pl.num_programs(0) - 1)(lambda: _wait_rows(row_copies(s1_ref, s2_ref, par), tm))


def _scatter_rows(h2, slot1, slot2, zero_tile_row, n_slots, tm, tmg):
    bsz, t_len, d = h2.shape
    n = bsz * t_len // tm

    def slot_spec(shift):
        return pl.BlockSpec((None, 1, tm), lambda i: (jnp.clip(i + shift, 0, n - 1), 0, 0),
                            memory_space=pltpu.SMEM)

    return pl.pallas_call(
        _scatter_kernel,
        grid=(n,),
        in_specs=[slot_spec(0), slot_spec(0), slot_spec(-1), slot_spec(-1),
                  pl.BlockSpec(memory_space=pltpu.SMEM),
                  pl.BlockSpec((tm, d), lambda i: (i, 0))],
        out_specs=pl.BlockSpec(memory_space=pl.ANY),
        out_shape=jax.ShapeDtypeStruct((n_slots,) + ROW_TILE, F32),
        scratch_shapes=[pltpu.VMEM((2, tm) + ROW_TILE, F32), pltpu.VMEM((tmg,) + ROW_TILE, F32),
                        pltpu.SemaphoreType.DMA((2,)), pltpu.SemaphoreType.DMA],
        compiler_params=_params(("arbitrary",)),
        name="scatter_rows",
    )(slot1, slot2, slot1, slot2, zero_tile_row, h2.reshape(bsz * t_len, d))


def _combine_kernel(s1_ref, s2_ref, n1_ref, n2_ref, route_ref, x_ref, mod_ref, gain_ref, y_ref,
                    o_ref, buf1, buf2, sems):
    tm, d = x_ref.shape
    i = pl.program_id(0)
    par = i % 2

    def row_copies(slot_a, slot_b, p):
        def copies(r):
            return (_row_copy(y_ref.at[pl.ds(slot_a[0, r], 1)], buf1.at[p, pl.ds(r, 1)], sems.at[p]),
                    _row_copy(y_ref.at[pl.ds(slot_b[0, r], 1)], buf2.at[p, pl.ds(r, 1)], sems.at[p]))
        return copies

    pl.when(i == 0)(lambda: _start_rows(row_copies(s1_ref, s2_ref, par), tm))
    pl.when(i + 1 < pl.num_programs(0))(
        lambda: _start_rows(row_copies(n1_ref, n2_ref, 1 - par), tm))
    _wait_rows(row_copies(s1_ref, s2_ref, par), tm)
    weights = route_ref[...].T
    f_out = (weights[:, 4:5] * buf1[par].reshape(tm, d)
             + weights[:, 5:6] * buf2[par].reshape(tm, d))
    o_ref[...] = x_ref[...] + mod_ref[0] * _rms(f_out, gain_ref[...])


def _combine_rows(y_sorted, slot1, slot2, route, x, mod_l, gain, tm):
    bsz, t_len, d = x.shape
    nt = t_len // tm
    n = bsz * nt

    def slot_spec(shift):
        return pl.BlockSpec((None, 1, tm), lambda i: (jnp.minimum(i + shift, n - 1), 0, 0),
                            memory_space=pltpu.SMEM)

    tok = pl.BlockSpec((None, tm, d), lambda i: (i // nt, i % nt, 0))
    return pl.pallas_call(
        _combine_kernel,
        grid=(n,),
        in_specs=[slot_spec(0), slot_spec(0), slot_spec(1), slot_spec(1),
                  pl.BlockSpec((None, None, N_EXPERTS, tm), lambda i: (i // nt, i % nt, 0, 0)),
                  tok,
                  pl.BlockSpec((1, None, 1, d), lambda i: (N_MOD - 1, i // nt, 0, 0)),
                  pl.BlockSpec((1, d), lambda i: (0, 0)),
                  pl.BlockSpec(memory_space=pl.ANY)],
        out_specs=tok,
        out_shape=jax.ShapeDtypeStruct((bsz, t_len, d), F32),
        scratch_shapes=[pltpu.VMEM((2, tm) + ROW_TILE, F32), pltpu.VMEM((2, tm) + ROW_TILE, F32),
                        pltpu.SemaphoreType.DMA((2,))],
        compiler_params=_params(("arbitrary",)),
        name="combine_rows",
    )(slot1, slot2, slot1, slot2, route, x, mod_l, gain, y_sorted)


def _moe_ffn(h2, route, counts, wg, wu, wd, first_expert, x, mod_l, gain, tm, tmg):
    bsz, t_len, d = x.shape
    n_tok = bsz * t_len
    n_tiles = (2 * n_tok) // tmg + N_EXPERTS
    e1, e2, r1, r2 = (route[:, :, j, :].astype(jnp.int32).reshape(n_tok) for j in range(4))
    cnt = counts[:, 0].astype(jnp.int32)
    tiles_per = (cnt + tmg - 1) // tmg
    tile_end = jnp.cumsum(tiles_per)
    offs = (tile_end - tiles_per) * tmg
    n_used = tile_end[-1:]
    tile_ids = jnp.arange(n_tiles, dtype=jnp.int32)
    tile_expert = jnp.sum(tile_ids[:, None] >= tile_end[None, :], axis=1).astype(jnp.int32)
    last_expert = jnp.sum(n_used - 1 >= tile_end).astype(jnp.int32)
    tile_expert = jnp.where(tile_ids < n_used, tile_expert, last_expert)
    zero_tile_row = jnp.where(cnt > 0, (tile_end - 1) * tmg, -1).astype(jnp.int32)
    nt = t_len // tm
    slot1 = (offs[e1] + r1).reshape(bsz * nt, 1, tm)
    slot2 = (offs[e2] + r2).reshape(bsz * nt, 1, tm)
    x_sorted = _scatter_rows(h2, slot1, slot2, zero_tile_row, n_tiles * tmg, tm, tmg)
    y_sorted = _ffn_grouped(x_sorted, tile_expert + first_expert, n_used, wg, wu, wd, tmg)
    return _combine_rows(y_sorted, slot1, slot2, route, x, mod_l, gain, tm)


def _rope_tables(pos):
    half = DK_B // 2
    inv = ROPE_BASE ** (-jnp.arange(half, dtype=F32) / half)
    ang = pos.astype(F32)[:, None] * inv[None, :]
    cos, sin = jnp.cos(ang), jnp.sin(ang)
    return jnp.concatenate([cos, cos], axis=-1), jnp.concatenate([-sin, sin], axis=-1)


def _trunk(x, mod, pos, s_a_in, s_b_in, lower_bounds, wts, tm, tb, tmg):
    depth = mod.shape[0]
    bsz = x.shape[0]
    cos_t, sin_t = _rope_tables(pos)
    sa_all = jnp.zeros((depth, bsz, H_A, DK_A, DV_A), F32)
    sb_all = jnp.zeros((depth, bsz, H_B, DK_B, DV_B), F32)
    for l in range(depth):
        mod_l = mod[l]
        gains_l = wts["norm_gains"][l]
        if s_a_in is None:
            s_a = jnp.zeros((bsz, H_A, DK_A, DV_A), F32)
            s_b = jnp.zeros((bsz, H_B, DK_B, DV_B), F32)
        else:
            s_a, s_b = s_a_in[l], s_b_in[l]
        proj, g_hi, g_lo = _in_projection(
            x, mod_l, gains_l[0:1], lower_bounds[l:l + 1],
            wts["g_norm_a"][l].reshape(1, H_A * DV_A), wts["g_norm_b"][l].reshape(1, H_B * DV_B),
            cos_t, sin_t, wts["w_in"][l], tm)
        o_a, o_b, sa_all, sb_all = _recurrences(proj, g_hi, g_lo, s_a, s_b, sa_all, sb_all, l, tb)
        j = l // 2
        router = None if l % 2 == 0 else (wts["w_router"][j], wts["b_router"][j])
        outs = _merge(o_a, o_b, proj, x, mod_l, gains_l, wts["w_branch_a"][l],
                      wts["w_branch_b"][l], wts["w_out"][l], router, tm)
        if l % 2 == 0:
            x, h2 = outs
            x = _ffn_dense(h2, wts["w_gate_dense"][j], wts["w_up_dense"][j],
                           wts["w_down_dense"][j], x, mod_l, gains_l[3:4], tm)
        else:
            x, h2, route, counts = outs
            x = _moe_ffn(h2, route, counts, wts["w_gate_exp"], wts["w_up_exp"],
                         wts["w_down_exp"], j * N_EXPERTS, x, mod_l, gains_l[3:4], tm, tmg)
    return x, sa_all, sb_all


def kernel(x_prompt, x_sample, state_hgrn, state_ret, c_prompt, c_sample, w_mod, b_mod, norm_gains,
           w_in, lb_logits, g_norm_a, g_norm_b, w_branch_a, w_branch_b, w_out, w_gate_dense,
           w_up_dense, w_down_dense, w_router, b_router, w_gate_exp, w_up_exp, w_down_exp):
    n_prompt = x_prompt.shape[0]
    past_len = 1024
    wts = dict(
        norm_gains=norm_gains, g_norm_a=g_norm_a, g_norm_b=g_norm_b,
        w_router=w_router, b_router=b_router,
        w_in=w_in.astype(BF16), w_branch_a=w_branch_a.astype(BF16),
        w_branch_b=w_branch_b.astype(BF16), w_out=w_out.astype(BF16),
        w_gate_dense=w_gate_dense.astype(BF16), w_up_dense=w_up_dense.astype(BF16),
        w_down_dense=w_down_dense.astype(BF16),
        w_gate_exp=w_gate_exp.astype(BF16).reshape((-1,) + w_gate_exp.shape[2:]),
        w_up_exp=w_up_exp.astype(BF16).reshape((-1,) + w_up_exp.shape[2:]),
        w_down_exp=w_down_exp.astype(BF16).reshape((-1,) + w_down_exp.shape[2:]))
    lb_p = jax.nn.softmax(lb_logits.astype(F32), axis=0)
    lower_bounds = jnp.clip(jnp.cumsum(lb_p, axis=0) - lb_p[:1], 0.0, 1.0)
    c_all = jnp.concatenate([c_prompt, c_sample], axis=0)
    mod = _modulation(c_all, w_mod, b_mod)
    mod = mod.reshape(mod.shape[:3] + (1, D_MODEL))
    pos_prompt = jnp.arange(x_prompt.shape[1], dtype=jnp.int32)
    pos_sample = past_len + jnp.arange(x_sample.shape[1], dtype=jnp.int32)
    t_p, t_s = x_prompt.shape[1], x_sample.shape[1]
    y_p, hg_p, rt_p = _trunk(x_prompt, mod[:, :, :n_prompt], pos_prompt, None, None,
                             lower_bounds, wts, tm=min(512, t_p), tb=min(1024, t_p), tmg=512)
    y_s, hg_s, rt_s = _trunk(x_sample, mod[:, :, n_prompt:], pos_sample, state_hgrn, state_ret,
                             lower_bounds, wts, tm=min(512, t_s), tb=min(512, t_s), tmg=64)
    return (y_p, y_s, hg_p, rt_p, hg_s, rt_s)
```

```python
import functools
import math

import jax
import jax.numpy as jnp
from jax import lax
from jax.experimental import pallas as pl
from jax.experimental.pallas import tpu as pltpu

F32 = jnp.float32
BF16 = jnp.bfloat16

D_MODEL = 1024
H_A, DK_A, DV_A = 4, 128, 256
H_B, DK_B, DV_B = 4, 128, 256
ROPE_BASE = 10000.0
FFN_DIM = 2816
N_EXPERTS = 8
RMS_EPS = 1e-6
F_FLOOR = 1e-20
N_MOD = 6
N_IN = 8 * D_MODEL
OFF_QA, OFF_KA, OFF_IA, OFF_OGA = 0, 512, 1024, 2048
OFF_QB, OFF_KB, OFF_VB, OFF_SGB = 3072, 3584, 4096, 5120
OFF_MGA, OFF_MGB = 6144, 7168
REC_COLS = 6144
SUB = 8
LOG2_GAMMA = tuple(math.log2(1.0 - 2.0 ** (-5.0 - h)) for h in range(H_B))
VMEM_LIMIT = 56 * 1024 * 1024
MXU_N = 256
ROW_TILE = (8, 128)


def _params(sem):
    return pltpu.CompilerParams(dimension_semantics=sem, vmem_limit_bytes=VMEM_LIMIT)


def _resident(shape, index_map):
    return pl.BlockSpec(shape, index_map, pipeline_mode=pl.Buffered(1))


def _rms(x, gain):
    return x * lax.rsqrt(jnp.mean(x * x, axis=-1, keepdims=True) + RMS_EPS) * gain


def _unit_rms(x):
    return x * lax.rsqrt(jnp.mean(x * x, axis=-1, keepdims=True) + RMS_EPS)


def _sigmoid(x):
    return 1.0 / (1.0 + jnp.exp(-x))


def _dot(a, b):
    return jnp.dot(a, b, preferred_element_type=F32)


def _dot_nt(a, b):
    return lax.dot_general(a, b, (((1,), (1,)), ((), ())), preferred_element_type=F32)


def _dot_tn(a, b):
    return lax.dot_general(a, b, (((0,), (0,)), ((), ())), preferred_element_type=F32)


def _split_bf16(x):
    hi = x.astype(BF16)
    return hi, (x - hi.astype(F32)).astype(BF16)


def _mod_kernel(c_ref, w_ref, b_ref, o_ref):
    c = c_ref[...]
    c_act = (c * _sigmoid(c)).astype(BF16)
    o_ref[...] = _dot(c_act, w_ref[...].astype(BF16)) + b_ref[...]


def _modulation(c_all, w_mod, b_mod):
    depth = w_mod.shape[0]
    n_seq = c_all.shape[0]
    d = D_MODEL
    return pl.pallas_call(
        _mod_kernel,
        grid=(depth, N_MOD),
        in_specs=[
            pl.BlockSpec((n_seq, d), lambda l, j: (0, 0)),
            pl.BlockSpec((None, d, d), lambda l, j: (l, 0, j)),
            pl.BlockSpec((None, None, 1, d), lambda l, j: (l, j, 0, 0)),
        ],
        out_specs=pl.BlockSpec((None, None, n_seq, d), lambda l, j: (l, j, 0, 0)),
        out_shape=jax.ShapeDtypeStruct((depth, N_MOD, n_seq, d), F32),
        compiler_params=_params(("parallel", "parallel")),
        name="modulation",
    )(c_all, w_mod, b_mod.reshape(depth, N_MOD, 1, d))


def _inproj_kernel(x_ref, mod_ref, gain_ref, lb_ref, gna_ref, gnb_ref, cos_ref, sin_ref, w_ref,
                   o_ref, ghi_ref, glo_ref):
    d = D_MODEL
    h = _rms(x_ref[...], gain_ref[...]) * (1.0 + mod_ref[1]) + mod_ref[0]
    hb = h.astype(BF16)

    def seg(cols):
        return _dot(hb, w_ref[:, cols[0]:cols[0] + cols[1]])

    def hgrn_q(q):
        o_ref[:, OFF_QA:OFF_QA + 512] = (q * _sigmoid(q)).astype(BF16)

    def hgrn_f(f_raw):
        lb = lb_ref[...]
        sig = _sigmoid(f_raw)
        g2 = jnp.log2(jnp.maximum(lb + (1.0 - lb) * sig, F_FLOOR))
        ghi, glo = _split_bf16(g2)
        ghi_ref[...] = ghi
        glo_ref[...] = glo
        o_ref[:, OFF_KA:OFF_KA + 512] = ((1.0 - lb) * (1.0 - sig)).astype(BF16)

    def retention_qk(y):
        cos, sin = cos_ref[...], sin_ref[...]
        for hh in range(H_B):
            xq = y[:, hh * DK_B:(hh + 1) * DK_B]
            xk = y[:, 512 + hh * DK_B:512 + (hh + 1) * DK_B]
            rq = xq * cos + pltpu.roll(xq, DK_B // 2, axis=1) * sin
            rk = (xk * cos + pltpu.roll(xk, DK_B // 2, axis=1) * sin) * (DK_B ** -0.5)
            o_ref[:, OFF_QB + hh * DK_B:OFF_QB + (hh + 1) * DK_B] = rq.astype(BF16)
            o_ref[:, OFF_KB + hh * DK_B:OFF_KB + (hh + 1) * DK_B] = rk.astype(BF16)

    def store(off, act):
        def epilogue(y):
            o_ref[:, off:off + d] = act(y).astype(BF16)
        return epilogue

    plan = [((1 * d, d), store(OFF_IA, lambda y: y)), ((512, 512), hgrn_f),
            ((2 * d, d), store(OFF_OGA, lambda y: _sigmoid(y) * gna_ref[...])), ((0, 512), hgrn_q),
            ((3 * d, d), retention_qk),
            ((5 * d, d), store(OFF_SGB, lambda y: y * _sigmoid(y) * gnb_ref[...])),
            ((6 * d, d), store(OFF_MGA, _sigmoid)), ((7 * d, d), store(OFF_MGB, _sigmoid)),
            ((4 * d, d), store(OFF_VB, lambda y: y))]
    y = seg(plan[0][0])
    for i, (_, epilogue) in enumerate(plan):
        y_next = seg(plan[i + 1][0]) if i + 1 < len(plan) else None
        epilogue(y)
        y = y_next


def _in_projection(x, mod_l, gain, lb, gna, gnb, cos_t, sin_t, w_in_l, tm):
    bsz, t_len, d = x.shape
    g_spec = pl.BlockSpec((None, tm, H_A * DK_A), lambda b, i: (b, i, 0))
    g_shape = jax.ShapeDtypeStruct((bsz, t_len, H_A * DK_A), BF16)
    return pl.pallas_call(
        _inproj_kernel,
        grid=(bsz, t_len // tm),
        in_specs=[
            pl.BlockSpec((None, tm, d), lambda b, i: (b, i, 0)),
            pl.BlockSpec((3, None, 1, d), lambda b, i: (0, b, 0, 0)),
            pl.BlockSpec((1, d), lambda b, i: (0, 0)),
            pl.BlockSpec((1, H_A * DK_A), lambda b, i: (0, 0)),
            pl.BlockSpec((1, d), lambda b, i: (0, 0)),
            pl.BlockSpec((1, d), lambda b, i: (0, 0)),
            pl.BlockSpec((tm, DK_B), lambda b, i: (i, 0)),
            pl.BlockSpec((tm, DK_B), lambda b, i: (i, 0)),
            _resident((d, N_IN), lambda b, i: (0, 0)),
        ],
        out_specs=[pl.BlockSpec((None, tm, N_IN), lambda b, i: (b, i, 0)), g_spec, g_spec],
        out_shape=[jax.ShapeDtypeStruct((bsz, t_len, N_IN), BF16), g_shape, g_shape],
        compiler_params=_params(("parallel", "parallel")),
        name="in_projection",
    )(x, mod_l, gain, lb, gna, gnb, cos_t, sin_t, w_in_l)


def _hgrn_diag_lhs(q, k, gc, x_s, row0, l_a):
    pack = 2 * SUB
    for m in range(l_a // pack):
        pieces = []
        for s in range(SUB):
            halves = []
            for lo in (m * pack, m * pack + SUB):
                decay = jnp.exp2(jnp.minimum(gc[lo:lo + SUB] - gc[lo + s:lo + s + 1], 0.0))
                halves.append(q[lo:lo + SUB] * decay * k[lo + s:lo + s + 1])
            pieces.append(jnp.concatenate(halves, axis=0).astype(BF16))
        x_s[row0 + m * pack:row0 + (m + 1) * pack, :] = jnp.concatenate(pieces, axis=1)


def _hgrn_off_diag(q, k, gc, l_a):
    q_parts, k_parts = [], []
    for j in range(l_a // SUB - 1):
        lo, hi = j * SUB, (j + 1) * SUB
        g_ref = gc[hi - 1:hi]
        q_hat = q[hi:] * jnp.exp2(gc[hi:] - g_ref)
        k_bar = k[lo:hi] * jnp.exp2(g_ref - gc[lo:hi])
        q_parts.append(jnp.concatenate([jnp.zeros((hi, DK_A), F32), q_hat], axis=0))
        k_rows = [k_bar, jnp.zeros((l_a - hi, DK_A), F32)]
        if lo:
            k_rows.insert(0, jnp.zeros((lo, DK_A), F32))
        k_parts.append(jnp.concatenate(k_rows, axis=0))
    return _dot_nt(jnp.concatenate(q_parts, axis=1).astype(BF16),
                   jnp.concatenate(k_parts, axis=1).astype(BF16))


def _recurrence_kernel(proj_ref, ghi_ref, glo_ref, bsel_ref, sa0_ref, sb0_ref,
                       sa_all_ref, sb_all_ref, oa_ref, ob_ref, sa_out_ref, sb_out_ref,
                       sa_s, sb_s, x_s, a_sc, qt_sc, pk_sc, *, tb, l_a, l_b):
    del sa_all_ref, sb_all_ref
    t = pl.program_id(1)

    @pl.when(t == 0)
    def _():
        sa_s[...] = sa0_ref[...]
        sb_s[...] = sb0_ref[...]

    ri = lax.broadcasted_iota(jnp.int32, (l_a, l_a), 0)
    ci = lax.broadcasted_iota(jnp.int32, (l_a, l_a), 1)
    tril = (ri >= ci).astype(BF16)
    diag_mask = ((ri // SUB) == (ci // SUB)) & (ri >= ci)

    def chunk_rows(c):
        return pl.ds(c * l_a, l_a)

    def stage1_prep(c):
        rows = chunk_rows(c)
        gc_all = _dot(tril, ghi_ref[rows, :]) + _dot(tril, glo_ref[rows, :])
        q_all = proj_ref[rows, OFF_QA:OFF_QA + H_A * DK_A].astype(F32)
        k_all = proj_ref[rows, OFF_KA:OFF_KA + H_A * DK_A].astype(F32)
        return gc_all, q_all, k_all

    def stage1_head(prep, h, slot):
        cols = slice(h * DK_A, (h + 1) * DK_A)
        gc, q, k = (a[:, cols] for a in prep)
        _hgrn_diag_lhs(q, k, gc, x_s, h * l_a, l_a)
        a = _dot(x_s[h * l_a:(h + 1) * l_a, :], bsel_ref[...])
        off = _hgrn_off_diag(q, k, gc, l_a) if l_a > SUB else 0.0
        a_sc[slot, h] = jnp.where(diag_mask, a, off).astype(BF16)
        qt_sc[slot, h] = (q * jnp.exp2(gc)).astype(BF16)
        g_end = gc[l_a - 1:l_a]
        k_tail = k * jnp.exp2(g_end - gc)
        packed = jnp.concatenate(
            [k_tail, jnp.broadcast_to(jnp.exp2(g_end), (DK_A - l_a, DK_A))], axis=0)
        pk_sc[slot, h] = packed.T

    def stage2_head(c, h, slot):
        rows = chunk_rows(c)
        v = proj_ref[rows, OFF_IA + h * DV_A:OFF_IA + (h + 1) * DV_A]
        og = proj_ref[rows, OFF_OGA + h * DV_A:OFF_OGA + (h + 1) * DV_A]
        s_prev = sa_s[h]
        o = _dot(a_sc[slot, h], v) + _dot(qt_sc[slot, h], s_prev.astype(BF16))
        packed_t = pk_sc[slot, h]
        sa_s[h] = (s_prev * packed_t[:, l_a:l_a + 1]
                   + _dot(packed_t[:, :l_a].astype(BF16), v))
        oa_ref[rows, h * DV_A:(h + 1) * DV_A] = _unit_rms(o).astype(BF16) * og

    ti = lax.broadcasted_iota(jnp.int32, (l_b, l_b), 0)
    si = lax.broadcasted_iota(jnp.int32, (l_b, l_b), 1)
    rel = (ti - si).astype(F32)
    tcol = lax.broadcasted_iota(jnp.int32, (l_b, 1), 0).astype(F32)
    decays = [jnp.where(ti >= si, jnp.exp2(jnp.minimum(rel * lg, 0.0)), 0.0) for lg in LOG2_GAMMA]
    q_scale = [jnp.exp2((tcol + 1.0) * lg) for lg in LOG2_GAMMA]
    k_scale = [jnp.exp2((l_b - 1.0 - tcol) * lg) for lg in LOG2_GAMMA]

    def retention_head(c, h):
        rows = pl.ds(c * l_b, l_b)
        qb = proj_ref[rows, OFF_QB + h * DK_B:OFF_QB + (h + 1) * DK_B]
        kb = proj_ref[rows, OFF_KB + h * DK_B:OFF_KB + (h + 1) * DK_B]
        v = proj_ref[rows, OFF_VB + h * DV_B:OFF_VB + (h + 1) * DV_B]
        sg = proj_ref[rows, OFF_SGB + h * DV_B:OFF_SGB + (h + 1) * DV_B]
        scores = _dot_nt(qb, kb) * decays[h]
        s_prev = sb_s[h]
        o = (_dot(scores.astype(BF16), v)
             + _dot((qb.astype(F32) * q_scale[h]).astype(BF16), s_prev.astype(BF16)))
        k_tail = (kb.astype(F32) * k_scale[h]).astype(BF16)
        sb_s[h] = s_prev * (2.0 ** (l_b * LOG2_GAMMA[h])) + _dot_tn(k_tail, v)
        ob_ref[rows, h * DV_B:(h + 1) * DV_B] = _unit_rms(o).astype(BF16) * sg

    n_a = tb // l_a
    prep = stage1_prep(0)
    for h in range(H_A):
        stage1_head(prep, h, 0)
    if n_a == 1:
        for h in range(H_A):
            stage2_head(0, h, 0)
            retention_head(0, h)
    else:
        for s in range(n_a // 2):
            prep = stage1_prep(2 * s + 1)
            for h in range(H_A):
                stage1_head(prep, h, 1)
                stage2_head(2 * s, h, 0)
            prep = stage1_prep(2 * s + 2) if 2 * s + 2 < n_a else None
            for h in range(H_A):
                if prep is not None:
                    stage1_head(prep, h, 0)
                stage2_head(2 * s + 1, h, 1)
                retention_head(s, h)

    @pl.when(t == pl.num_programs(1) - 1)
    def _():
        sa_out_ref[...] = sa_s[...]
        sb_out_ref[...] = sb_s[...]


def _diag_selector(l_a):
    s_of_row = jnp.arange(SUB * DK_A, dtype=jnp.int32) // DK_A
    col = jnp.arange(l_a, dtype=jnp.int32)
    return (s_of_row[:, None] == (col[None, :] % SUB)).astype(BF16)


def _recurrences(proj, g_hi, g_lo, s_a0, s_b0, sa_all, sb_all, layer, tb):
    bsz, t_len, _ = proj.shape
    l_a = min(64, t_len)
    l_b = min(128, t_len)
    state_spec = pl.BlockSpec((None, H_A, DK_A, DV_A), lambda b, t: (b, 0, 0, 0))
    state_out_spec = pl.BlockSpec((None, None, H_A, DK_A, DV_A), lambda b, t: (layer, b, 0, 0, 0))
    hbm = pl.BlockSpec(memory_space=pl.ANY)
    act_spec = pl.BlockSpec((None, tb, D_MODEL), lambda b, t: (b, t, 0))
    g_spec = pl.BlockSpec((None, tb, H_A * DK_A), lambda b, t: (b, t, 0))
    return pl.pallas_call(
        functools.partial(_recurrence_kernel, tb=tb, l_a=l_a, l_b=l_b),
        grid=(bsz, t_len // tb),
        in_specs=[
            pl.BlockSpec((None, tb, REC_COLS), lambda b, t: (b, t, 0)),
            g_spec, g_spec,
            pl.BlockSpec((SUB * DK_A, l_a), lambda b, t: (0, 0)),
            state_spec, state_spec, hbm, hbm,
        ],
        out_specs=[act_spec, act_spec, state_out_spec, state_out_spec],
        out_shape=[
            jax.ShapeDtypeStruct((bsz, t_len, D_MODEL), BF16),
            jax.ShapeDtypeStruct((bsz, t_len, D_MODEL), BF16),
            jax.ShapeDtypeStruct(sa_all.shape, F32),
            jax.ShapeDtypeStruct(sb_all.shape, F32),
        ],
        input_output_aliases={6: 2, 7: 3},
        scratch_shapes=[pltpu.VMEM((H_A, DK_A, DV_A), F32), pltpu.VMEM((H_B, DK_B, DV_B), F32),
                        pltpu.VMEM((H_A * l_a, SUB * DK_A), BF16),
                        pltpu.VMEM((2, H_A, l_a, l_a), BF16),
                        pltpu.VMEM((2, H_A, l_a, DK_A), BF16),
                        pltpu.VMEM((2, H_A, DK_A, DK_A), F32)],
        compiler_params=_params(("parallel", "arbitrary")),
        name="recurrences",
    )(proj, g_hi, g_lo, _diag_selector(l_a), s_a0, s_b0, sa_all, sb_all)


def _merge_kernel(*refs, with_router):
    if with_router:
        (oa_ref, ob_ref, mg_ref, x_ref, mod_ref, gains_ref, wba_ref, wbb_ref, wo_ref,
         wr_ref, br_ref, lt_ref, xo_ref, h2_ref, route_ref, cnt_ref, run_s) = refs
    else:
        (oa_ref, ob_ref, mg_ref, x_ref, mod_ref, gains_ref, wba_ref, wbb_ref, wo_ref,
         xo_ref, h2_ref) = refs
    d = D_MODEL
    tm = x_ref.shape[0]
    halves = [pl.ds(0, tm // 2), pl.ds(tm // 2, tm // 2)] if tm >= 256 else [pl.ds(0, tm)]
    merged = []
    for rows in halves:
        y_a = _dot(oa_ref[rows, :], wba_ref[...])
        y_b = _dot(ob_ref[rows, :], wbb_ref[...])
        merged.append((mg_ref[rows, :d].astype(F32) * y_a
                       + mg_ref[rows, d:].astype(F32) * y_b).astype(BF16))
    m_outs = [_dot(m, wo_ref[...]) for m in merged]
    if with_router:
        w_hi, w_lo = _split_bf16(wr_ref[...])
    logit_parts = []
    for rows, m_out in zip(halves, m_outs):
        x = x_ref[rows, :] + mod_ref[2] * _rms(m_out, gains_ref[1:2])
        xo_ref[rows, :] = x
        h2 = _rms(x, gains_ref[2:3]) * (1.0 + mod_ref[4]) + mod_ref[3]
        if not with_router:
            h2_ref[rows, :] = h2.astype(BF16)
            continue
        h2_ref[rows, :] = h2
        h_hi, h_lo = _split_bf16(h2)
        logit_parts.append(_dot_nt(w_hi, h_hi) + _dot_nt(w_lo, h_hi) + _dot_nt(w_hi, h_lo))
    if not with_router:
        return

    @pl.when((pl.program_id(0) == 0) & (pl.program_id(1) == 0))
    def _():
        run_s[...] = jnp.zeros_like(run_s)

    logits = jnp.concatenate(logit_parts, axis=1) + br_ref[...]
    exp_id = lax.broadcasted_iota(jnp.int32, logits.shape, 0)
    m1 = jnp.max(logits, axis=0, keepdims=True)
    i1 = jnp.min(jnp.where(logits == m1, exp_id, N_EXPERTS), axis=0, keepdims=True)
    rest = jnp.where(exp_id == i1, -jnp.inf, logits)
    m2 = jnp.max(rest, axis=0, keepdims=True)
    i2 = jnp.min(jnp.where(rest == m2, exp_id, N_EXPERTS), axis=0, keepdims=True)
    e2 = jnp.exp(m2 - m1)
    w1 = 1.0 / (1.0 + e2)
    w2 = e2 / (1.0 + e2)
    chosen = ((exp_id == i1) | (exp_id == i2)).astype(F32)
    before = run_s[...] + _dot(chosen.astype(BF16), lt_ref[...])
    r1 = jnp.sum(jnp.where(exp_id == i1, before, 0.0), axis=0, keepdims=True)
    r2 = jnp.sum(jnp.where(exp_id == i2, before, 0.0), axis=0, keepdims=True)
    run_s[...] = run_s[...] + jnp.sum(chosen, axis=1, keepdims=True)
    cnt_ref[...] = run_s[...]
    route = jnp.zeros(logits.shape, F32)
    for j, val in enumerate((i1.astype(F32), i2.astype(F32), r1, r2, w1, w2)):
        route = jnp.where(exp_id == j, val, route)
    route_ref[...] = route


def _merge(o_a, o_b, proj, x, mod_l, gains_l, wba, wbb, wo, router, tm):
    bsz, t_len, d = x.shape
    with_router = router is not None
    tok = pl.BlockSpec((None, tm, d), lambda b, i: (b, i, 0))
    in_specs = [
        tok, tok,
        pl.BlockSpec((None, tm, 2 * d), lambda b, i: (b, i, OFF_MGA // (2 * d))),
        tok,
        pl.BlockSpec((N_MOD, None, 1, d), lambda b, i: (0, b, 0, 0)),
        pl.BlockSpec((4, d), lambda b, i: (0, 0)),
        _resident((d, d), lambda b, i: (0, 0)),
        _resident((d, d), lambda b, i: (0, 0)),
        _resident((d, d), lambda b, i: (0, 0)),
    ]
    args = [o_a, o_b, proj, x, mod_l, gains_l, wba, wbb, wo]
    if not with_router:
        return pl.pallas_call(
            functools.partial(_merge_kernel, with_router=False),
            grid=(bsz, t_len // tm),
            in_specs=in_specs,
            out_specs=[tok, tok],
            out_shape=[jax.ShapeDtypeStruct((bsz, t_len, d), F32),
                       jax.ShapeDtypeStruct((bsz, t_len, d), BF16)],
            compiler_params=_params(("parallel", "parallel")),
            name="merge",
        )(*args)
    w_r, b_r = router
    earlier = jnp.triu(jnp.ones((tm, tm), BF16), 1)
    in_specs += [pl.BlockSpec((N_EXPERTS, d), lambda b, i: (0, 0)),
                 pl.BlockSpec((N_EXPERTS, 1), lambda b, i: (0, 0)),
                 pl.BlockSpec((tm, tm), lambda b, i: (0, 0))]
    args += [w_r.T, b_r.reshape(N_EXPERTS, 1), earlier]
    nt = t_len // tm
    return pl.pallas_call(
        functools.partial(_merge_kernel, with_router=True),
        grid=(bsz, nt),
        in_specs=in_specs,
        out_specs=[tok, tok,
                   pl.BlockSpec((None, None, N_EXPERTS, tm), lambda b, i: (b, i, 0, 0)),
                   pl.BlockSpec((N_EXPERTS, 1), lambda b, i: (0, 0))],
        out_shape=[jax.ShapeDtypeStruct((bsz, t_len, d), F32),
                   jax.ShapeDtypeStruct((bsz, t_len, d), F32),
                   jax.ShapeDtypeStruct((bsz, nt, N_EXPERTS, tm), F32),
                   jax.ShapeDtypeStruct((N_EXPERTS, 1), F32)],
        scratch_shapes=[pltpu.VMEM((N_EXPERTS, 1), F32)],
        compiler_params=_params(("arbitrary", "arbitrary")),
        name="merge_router",
    )(*args)


def _swiglu_hidden(hb, wg_ref, wu_ref, a_s):
    for c in range(FFN_DIM // MXU_N):
        sl = slice(c * MXU_N, (c + 1) * MXU_N)
        gate = _dot(hb, wg_ref[:, sl])
        up = _dot(hb, wu_ref[:, sl])
        a_s[:, sl] = (gate * _sigmoid(gate) * up).astype(BF16)


def _ffn_dense_kernel(h_ref, wg_ref, wu_ref, wd_ref, x_ref, mod_ref, gain_ref, o_ref, a_s):
    _swiglu_hidden(h_ref[...], wg_ref, wu_ref, a_s)
    y = _dot(a_s[...], wd_ref[...])
    o_ref[...] = x_ref[...] + mod_ref[0] * _rms(y, gain_ref[...])


def _ffn_dense(h2, wg, wu, wd, x, mod_l, gain, tm):
    bsz, t_len, d = x.shape
    tok = pl.BlockSpec((None, tm, d), lambda b, i: (b, i, 0))
    return pl.pallas_call(
        _ffn_dense_kernel,
        grid=(bsz, t_len // tm),
        in_specs=[
            tok,
            _resident((d, FFN_DIM), lambda b, i: (0, 0)),
            _resident((d, FFN_DIM), lambda b, i: (0, 0)),
            _resident((FFN_DIM, d), lambda b, i: (0, 0)),
            tok,
            pl.BlockSpec((1, None, 1, d), lambda b, i: (N_MOD - 1, b, 0, 0)),
            pl.BlockSpec((1, d), lambda b, i: (0, 0)),
        ],
        out_specs=tok,
        out_shape=jax.ShapeDtypeStruct((bsz, t_len, d), F32),
        scratch_shapes=[pltpu.VMEM((tm, FFN_DIM), BF16)],
        compiler_params=_params(("parallel", "parallel")),
        name="ffn_dense",
    )(h2, wg, wu, wd, x, mod_l, gain)


def _ffn_grouped_kernel(te_ref, nu_ref, x_ref, wg_ref, wu_ref, wd_ref, y_ref, a_s):
    del te_ref

    @pl.when(pl.program_id(0) < nu_ref[0])
    def _():
        tmg = x_ref.shape[0]
        x = x_ref[...].reshape(tmg, D_MODEL)
        _swiglu_hidden(x.astype(BF16), wg_ref, wu_ref, a_s)
        y_ref[...] = _dot(a_s[...], wd_ref[...]).reshape(y_ref.shape)


def _ffn_grouped(x_sorted, tile_expert, n_used, wg, wu, wd, tmg):
    n_slots = x_sorted.shape[0]
    d = D_MODEL
    n_tiles = n_slots // tmg

    def row_map(i, te, nu):
        return (jnp.minimum(i, nu[0] - 1), 0, 0)

    def w_map(i, te, nu):
        return (te[i], 0, 0)

    return pl.pallas_call(
        _ffn_grouped_kernel,
        grid_spec=pltpu.PrefetchScalarGridSpec(
            num_scalar_prefetch=2,
            grid=(n_tiles,),
            in_specs=[
                pl.BlockSpec((tmg,) + ROW_TILE, row_map),
                pl.BlockSpec((None, d, FFN_DIM), w_map),
                pl.BlockSpec((None, d, FFN_DIM), w_map),
                pl.BlockSpec((None, FFN_DIM, d), w_map),
            ],
            out_specs=pl.BlockSpec((tmg,) + ROW_TILE, row_map),
            scratch_shapes=[pltpu.VMEM((tmg, FFN_DIM), BF16)],
        ),
        out_shape=jax.ShapeDtypeStruct((n_slots,) + ROW_TILE, F32),
        compiler_params=_params(("arbitrary",)),
        name="ffn_grouped",
    )(tile_expert, n_used, x_sorted, wg, wu, wd)


def _row_copy(src_ref, dst_ref, sem):
    return pltpu.make_async_copy(src_ref, dst_ref, sem)


def _start_rows(copies, n_rows):
    def issue(r, carry):
        for i, cp in enumerate(copies(r)):
            cp.start(priority=i % 2)
        return carry

    lax.fori_loop(0, n_rows, issue, 0, unroll=8)


def _wait_rows(copies, n_rows):
    def drain(r, carry):
        for cp in copies(r):
            cp.wait()
        return carry

    lax.fori_loop(0, n_rows, drain, 0, unroll=8)


def _scatter_kernel(s1_ref, s2_ref, p1_ref, p2_ref, ztile_ref, h_ref, xs_ref,
                    rows_s, zero_s, sems, zsem):
    tm = h_ref.shape[0]
    tmg = zero_s.shape[0]
    i = pl.program_id(0)
    par = i % 2

    @pl.when(i == 0)
    def _():
        zero_s[...] = jnp.zeros_like(zero_s)

        def zero_copy(e):
            return pltpu.make_async_copy(zero_s, xs_ref.at[pl.ds(ztile_ref[e], tmg)], zsem)

        for e in range(N_EXPERTS):
            pl.when(ztile_ref[e] >= 0)(lambda e=e: zero_copy(e).start())
        for e in range(N_EXPERTS):
            pl.when(ztile_ref[e] >= 0)(lambda e=e: zero_copy(e).wait())

    def row_copies(slot_a, slot_b, p):
        def copies(r):
            src = rows_s.at[p, pl.ds(r, 1)]
            return (_row_copy(src, xs_ref.at[pl.ds(slot_a[0, r], 1)], sems.at[p]),
                    _row_copy(src, xs_ref.at[pl.ds(slot_b[0, r], 1)], sems.at[p]))
        return copies

    rows_s[par] = h_ref[...].reshape(rows_s.shape[1:])
    _start_rows(row_copies(s1_ref, s2_ref, par), tm)
    pl.when(i > 0)(lambda: _wait_rows(row_copies(p1_ref, p2_ref, 1 - par), tm))
    pl.when(i == pl.num_programs(0) - 1)(lambda: _wait_rows(row_copies(s1_ref, s2_ref, par), tm))


def _scatter_rows(h2, slot1, slot2, zero_tile_row, n_slots, tm, tmg):
    bsz, t_len, d = h2.shape
    n = bsz * t_len // tm

    def slot_spec(shift):
        return pl.BlockSpec((None, 1, tm), lambda i: (jnp.clip(i + shift, 0, n - 1), 0, 0),
                            memory_space=pltpu.SMEM)

    return pl.pallas_call(
        _scatter_kernel,
        grid=(n,),
        in_specs=[slot_spec(0), slot_spec(0), slot_spec(-1), slot_spec(-1),
                  pl.BlockSpec(memory_space=pltpu.SMEM),
                  pl.BlockSpec((tm, d), lambda i: (i, 0))],
        out_specs=pl.BlockSpec(memory_space=pl.ANY),
        out_shape=jax.ShapeDtypeStruct((n_slots,) + ROW_TILE, F32),
        scratch_shapes=[pltpu.VMEM((2, tm) + ROW_TILE, F32), pltpu.VMEM((tmg,) + ROW_TILE, F32),
                        pltpu.SemaphoreType.DMA((2,)), pltpu.SemaphoreType.DMA],
        compiler_params=_params(("arbitrary",)),
        name="scatter_rows",
    )(slot1, slot2, slot1, slot2, zero_tile_row, h2.reshape(bsz * t_len, d))


def _combine_kernel(s1_ref, s2_ref, n1_ref, n2_ref, route_ref, x_ref, mod_ref, gain_ref, y_ref,
                    o_ref, buf1, buf2, sems):
    tm, d = x_ref.shape
    i = pl.program_id(0)
    par = i % 2

    def row_copies(slot_a, slot_b, p):
        def copies(r):
            return (_row_copy(y_ref.at[pl.ds(slot_a[0, r], 1)], buf1.at[p, pl.ds(r, 1)], sems.at[p]),
                    _row_copy(y_ref.at[pl.ds(slot_b[0, r], 1)], buf2.at[p, pl.ds(r, 1)], sems.at[p]))
        return copies

    pl.when(i == 0)(lambda: _start_rows(row_copies(s1_ref, s2_ref, par), tm))
    pl.when(i + 1 < pl.num_programs(0))(
        lambda: _start_rows(row_copies(n1_ref, n2_ref, 1 - par), tm))
    _wait_rows(row_copies(s1_ref, s2_ref, par), tm)
    weights = route_ref[...].T
    f_out = (weights[:, 4:5] * buf1[par].reshape(tm, d)
             + weights[:, 5:6] * buf2[par].reshape(tm, d))
    o_ref[...] = x_ref[...] + mod_ref[0] * _rms(f_out, gain_ref[...])


def _combine_rows(y_sorted, slot1, slot2, route, x, mod_l, gain, tm):
    bsz, t_len, d = x.shape
    nt = t_len // tm
    n = bsz * nt

    def slot_spec(shift):
        return pl.BlockSpec((None, 1, tm), lambda i: (jnp.minimum(i + shift, n - 1), 0, 0),
                            memory_space=pltpu.SMEM)

    tok = pl.BlockSpec((None, tm, d), lambda i: (i // nt, i % nt, 0))
    return pl.pallas_call(
        _combine_kernel,
        grid=(n,),
        in_specs=[slot_spec(0), slot_spec(0), slot_spec(1), slot_spec(1),
                  pl.BlockSpec((None, None, N_EXPERTS, tm), lambda i: (i // nt, i % nt, 0, 0)),
                  tok,
                  pl.BlockSpec((1, None, 1, d), lambda i: (N_MOD - 1, i // nt, 0, 0)),
                  pl.BlockSpec((1, d), lambda i: (0, 0)),
                  pl.BlockSpec(memory_space=pl.ANY)],
        out_specs=tok,
        out_shape=jax.ShapeDtypeStruct((bsz, t_len, d), F32),
        scratch_shapes=[pltpu.VMEM((2, tm) + ROW_TILE, F32), pltpu.VMEM((2, tm) + ROW_TILE, F32),
                        pltpu.SemaphoreType.DMA((2,))],
        compiler_params=_params(("arbitrary",)),
        name="combine_rows",
    )(slot1, slot2, slot1, slot2, route, x, mod_l, gain, y_sorted)


def _moe_ffn(h2, route, counts, wg, wu, wd, first_expert, x, mod_l, gain, tm, tmg):
    bsz, t_len, d = x.shape
    n_tok = bsz * t_len
    n_tiles = (2 * n_tok) // tmg + N_EXPERTS
    e1, e2, r1, r2 = (route[:, :, j, :].astype(jnp.int32).reshape(n_tok) for j in range(4))
    cnt = counts[:, 0].astype(jnp.int32)
    tiles_per = (cnt + tmg - 1) // tmg
    tile_end = jnp.cumsum(tiles_per)
    offs = (tile_end - tiles_per) * tmg
    n_used = tile_end[-1:]
    tile_ids = jnp.arange(n_tiles, dtype=jnp.int32)
    tile_expert = jnp.sum(tile_ids[:, None] >= tile_end[None, :], axis=1).astype(jnp.int32)
    last_expert = jnp.sum(n_used - 1 >= tile_end).astype(jnp.int32)
    tile_expert = jnp.where(tile_ids < n_used, tile_expert, last_expert)
    zero_tile_row = jnp.where(cnt > 0, (tile_end - 1) * tmg, -1).astype(jnp.int32)
    nt = t_len // tm
    slot1 = (offs[e1] + r1).reshape(bsz * nt, 1, tm)
    slot2 = (offs[e2] + r2).reshape(bsz * nt, 1, tm)
    x_sorted = _scatter_rows(h2, slot1, slot2, zero_tile_row, n_tiles * tmg, tm, tmg)
    y_sorted = _ffn_grouped(x_sorted, tile_expert + first_expert, n_used, wg, wu, wd, tmg)
    return _combine_rows(y_sorted, slot1, slot2, route, x, mod_l, gain, tm)


def _rope_tables(pos):
    half = DK_B // 2
    inv = ROPE_BASE ** (-jnp.arange(half, dtype=F32) / half)
    ang = pos.astype(F32)[:, None] * inv[None, :]
    cos, sin = jnp.cos(ang), jnp.sin(ang)
    return jnp.concatenate([cos, cos], axis=-1), jnp.concatenate([-sin, sin], axis=-1)


def _trunk(x, mod, pos, s_a_in, s_b_in, lower_bounds, wts, tm, tb, tmg):
    depth = mod.shape[0]
    bsz = x.shape[0]
    cos_t, sin_t = _rope_tables(pos)
    sa_all = jnp.zeros((depth, bsz, H_A, DK_A, DV_A), F32)
    sb_all = jnp.zeros((depth, bsz, H_B, DK_B, DV_B), F32)
    for l in range(depth):
        mod_l = mod[l]
        gains_l = wts["norm_gains"][l]
        if s_a_in is None:
            s_a = jnp.zeros((bsz, H_A, DK_A, DV_A), F32)
            s_b = jnp.zeros((bsz, H_B, DK_B, DV_B), F32)
        else:
            s_a, s_b = s_a_in[l], s_b_in[l]
        proj, g_hi, g_lo = _in_projection(
            x, mod_l, gains_l[0:1], lower_bounds[l:l + 1],
            wts["g_norm_a"][l].reshape(1, H_A * DV_A), wts["g_norm_b"][l].reshape(1, H_B * DV_B),
            cos_t, sin_t, wts["w_in"][l], tm)
        o_a, o_b, sa_all, sb_all = _recurrences(proj, g_hi, g_lo, s_a, s_b, sa_all, sb_all, l, tb)
        j = l // 2
        router = None if l % 2 == 0 else (wts["w_router"][j], wts["b_router"][j])
        outs = _merge(o_a, o_b, proj, x, mod_l, gains_l, wts["w_branch_a"][l],
                      wts["w_branch_b"][l], wts["w_out"][l], router, tm)
        if l % 2 == 0:
            x, h2 = outs
            x = _ffn_dense(h2, wts["w_gate_dense"][j], wts["w_up_dense"][j],
                           wts["w_down_dense"][j], x, mod_l, gains_l[3:4], tm)
        else:
            x, h2, route, counts = outs
            x = _moe_ffn(h2, route, counts, wts["w_gate_exp"], wts["w_up_exp"],
                         wts["w_down_exp"], j * N_EXPERTS, x, mod_l, gains_l[3:4], tm, tmg)
    return x, sa_all, sb_all


def kernel(x_prompt, x_sample, state_hgrn, state_ret, c_prompt, c_sample, w_mod, b_mod, norm_gains,
           w_in, lb_logits, g_norm_a, g_norm_b, w_branch_a, w_branch_b, w_out, w_gate_dense,
           w_up_dense, w_down_dense, w_router, b_router, w_gate_exp, w_up_exp, w_down_exp):
    n_prompt = x_prompt.shape[0]
    past_len = 1024
    wts = dict(
        norm_gains=norm_gains, g_norm_a=g_norm_a, g_norm_b=g_norm_b,
        w_router=w_router, b_router=b_router,
        w_in=w_in.astype(BF16), w_branch_a=w_branch_a.astype(BF16),
        w_branch_b=w_branch_b.astype(BF16), w_out=w_out.astype(BF16),
        w_gate_dense=w_gate_dense.astype(BF16), w_up_dense=w_up_dense.astype(BF16),
        w_down_dense=w_down_dense.astype(BF16),
        w_gate_exp=w_gate_exp.astype(BF16).reshape((-1,) + w_gate_exp.shape[2:]),
        w_up_exp=w_up_exp.astype(BF16).reshape((-1,) + w_up_exp.shape[2:]),
        w_down_exp=w_down_exp.astype(BF16).reshape((-1,) + w_down_exp.shape[2:]))
    lb_p = jax.nn.softmax(lb_logits.astype(F32), axis=0)
    lower_bounds = jnp.clip(jnp.cumsum(lb_p, axis=0) - lb_p[:1], 0.0, 1.0)
    c_all = jnp.concatenate([c_prompt, c_sample], axis=0)
    mod = _modulation(c_all, w_mod, b_mod)
    mod = mod.reshape(mod.shape[:3] + (1, D_MODEL))
    pos_prompt = jnp.arange(x_prompt.shape[1], dtype=jnp.int32)
    pos_sample = past_len + jnp.arange(x_sample.shape[1], dtype=jnp.int32)
    t_p, t_s = x_prompt.shape[1], x_sample.shape[1]
    y_p, hg_p, rt_p = _trunk(x_prompt, mod[:, :, :n_prompt], pos_prompt, None, None,
                             lower_bounds, wts, tm=min(512, t_p), tb=min(1024, t_p), tmg=512)
    y_s, hg_s, rt_s = _trunk(x_sample, mod[:, :, n_prompt:], pos_sample, state_hgrn, state_ret,
                             lower_bounds, wts, tm=min(512, t_s), tb=min(512, t_s), tmg=64)
    return (y_p, y_s, hg_p, rt_p, hg_s, rt_s)
```

```python
import functools
import math

import jax
import jax.numpy as jnp
from jax import lax
from jax.experimental import pallas as pl
from jax.experimental.pallas import tpu as pltpu

F32 = jnp.float32
BF16 = jnp.bfloat16

D_MODEL = 1024
H_A, DK_A, DV_A = 4, 128, 256
H_B, DK_B, DV_B = 4, 128, 256
ROPE_BASE = 10000.0
FFN_DIM = 2816
N_EXPERTS = 8
RMS_EPS = 1e-6
F_FLOOR = 1e-20
N_MOD = 6
N_IN = 8 * D_MODEL
OFF_QA, OFF_KA, OFF_IA, OFF_OGA = 0, 512, 1024, 2048
OFF_QB, OFF_KB, OFF_VB, OFF_SGB = 3072, 3584, 4096, 5120
OFF_MGA, OFF_MGB = 6144, 7168
REC_COLS = 6144
SUB = 8
LOG2_GAMMA = tuple(math.log2(1.0 - 2.0 ** (-5.0 - h)) for h in range(H_B))
VMEM_LIMIT = 56 * 1024 * 1024
MXU_N = 256
ROW_TILE = (8, 128)


def _params(sem):
    return pltpu.CompilerParams(dimension_semantics=sem, vmem_limit_bytes=VMEM_LIMIT)


def _resident(shape, index_map):
    return pl.BlockSpec(shape, index_map, pipeline_mode=pl.Buffered(1))


def _rms(x, gain):
    return x * lax.rsqrt(jnp.mean(x * x, axis=-1, keepdims=True) + RMS_EPS) * gain


def _unit_rms(x):
    return x * lax.rsqrt(jnp.mean(x * x, axis=-1, keepdims=True) + RMS_EPS)


def _sigmoid(x):
    return 1.0 / (1.0 + jnp.exp(-x))


def _dot(a, b):
    return jnp.dot(a, b, preferred_element_type=F32)


def _dot_nt(a, b):
    return lax.dot_general(a, b, (((1,), (1,)), ((), ())), preferred_element_type=F32)


def _dot_tn(a, b):
    return lax.dot_general(a, b, (((0,), (0,)), ((), ())), preferred_element_type=F32)


def _split_bf16(x):
    hi = x.astype(BF16)
    return hi, (x - hi.astype(F32)).astype(BF16)


def _mod_kernel(c_ref, w_ref, b_ref, o_ref):
    c = c_ref[...]
    c_act = (c * _sigmoid(c)).astype(BF16)
    o_ref[...] = _dot(c_act, w_ref[...].astype(BF16)) + b_ref[...]


def _modulation(c_all, w_mod, b_mod):
    depth = w_mod.shape[0]
    n_seq = c_all.shape[0]
    d = D_MODEL
    return pl.pallas_call(
        _mod_kernel,
        grid=(depth, N_MOD),
        in_specs=[
            pl.BlockSpec((n_seq, d), lambda l, j: (0, 0)),
            pl.BlockSpec((None, d, d), lambda l, j: (l, 0, j)),
            pl.BlockSpec((None, None, 1, d), lambda l, j: (l, j, 0, 0)),
        ],
        out_specs=pl.BlockSpec((None, None, n_seq, d), lambda l, j: (l, j, 0, 0)),
        out_shape=jax.ShapeDtypeStruct((depth, N_MOD, n_seq, d), F32),
        compiler_params=_params(("parallel", "parallel")),
        name="modulation",
    )(c_all, w_mod, b_mod.reshape(depth, N_MOD, 1, d))


def _inproj_kernel(x_ref, mod_ref, gain_ref, lb_ref, gna_ref, gnb_ref, cos_ref, sin_ref, w_ref,
                   o_ref, ghi_ref, glo_ref):
    d = D_MODEL
    h = _rms(x_ref[...], gain_ref[...] * (1.0 + mod_ref[1])) + mod_ref[0]
    hb = h.astype(BF16)

    def seg(cols):
        return _dot(hb, w_ref[:, cols[0]:cols[0] + cols[1]])

    def hgrn_q(q):
        o_ref[:, OFF_QA:OFF_QA + 512] = (q * _sigmoid(q)).astype(BF16)

    def hgrn_f(f_raw):
        lb = lb_ref[...]
        open_part = (1.0 - lb) * _sigmoid(f_raw)
        g2 = jnp.log2(jnp.maximum(lb + open_part, F_FLOOR))
        ghi, glo = _split_bf16(g2)
        ghi_ref[...] = ghi
        glo_ref[...] = glo
        o_ref[:, OFF_KA:OFF_KA + 512] = ((1.0 - lb) - open_part).astype(BF16)

    def retention_qk(y):
        cos, sin = cos_ref[...], sin_ref[...]
        for hh in range(H_B):
            xq = y[:, hh * DK_B:(hh + 1) * DK_B]
            xk = y[:, 512 + hh * DK_B:512 + (hh + 1) * DK_B]
            rq = xq * cos + pltpu.roll(xq, DK_B // 2, axis=1) * sin
            rk = (xk * cos + pltpu.roll(xk, DK_B // 2, axis=1) * sin) * (DK_B ** -0.5)
            o_ref[:, OFF_QB + hh * DK_B:OFF_QB + (hh + 1) * DK_B] = rq.astype(BF16)
            o_ref[:, OFF_KB + hh * DK_B:OFF_KB + (hh + 1) * DK_B] = rk.astype(BF16)

    def store(off, act):
        def epilogue(y):
            o_ref[:, off:off + d] = act(y).astype(BF16)
        return epilogue

    plan = [((1 * d, d), store(OFF_IA, lambda y: y)), ((512, 512), hgrn_f),
            ((2 * d, d), store(OFF_OGA, lambda y: _sigmoid(y) * gna_ref[...])), ((0, 512), hgrn_q),
            ((3 * d, d), retention_qk),
            ((5 * d, d), store(OFF_SGB, lambda y: y * _sigmoid(y) * gnb_ref[...])),
            ((6 * d, d), store(OFF_MGA, _sigmoid)), ((7 * d, d), store(OFF_MGB, _sigmoid)),
            ((4 * d, d), store(OFF_VB, lambda y: y))]
    y = seg(plan[0][0])
    for i, (_, epilogue) in enumerate(plan):
        y_next = seg(plan[i + 1][0]) if i + 1 < len(plan) else None
        epilogue(y)
        y = y_next


def _in_projection(x, mod_l, gain, lb, gna, gnb, cos_t, sin_t, w_in_l, tm):
    bsz, t_len, d = x.shape
    g_spec = pl.BlockSpec((None, tm, H_A * DK_A), lambda b, i: (b, i, 0))
    g_shape = jax.ShapeDtypeStruct((bsz, t_len, H_A * DK_A), BF16)
    return pl.pallas_call(
        _inproj_kernel,
        grid=(bsz, t_len // tm),
        in_specs=[
            pl.BlockSpec((None, tm, d), lambda b, i: (b, i, 0)),
            pl.BlockSpec((3, None, 1, d), lambda b, i: (0, b, 0, 0)),
            pl.BlockSpec((1, d), lambda b, i: (0, 0)),
            pl.BlockSpec((1, H_A * DK_A), lambda b, i: (0, 0)),
            pl.BlockSpec((1, d), lambda b, i: (0, 0)),
            pl.BlockSpec((1, d), lambda b, i: (0, 0)),
            pl.BlockSpec((tm, DK_B), lambda b, i: (i, 0)),
            pl.BlockSpec((tm, DK_B), lambda b, i: (i, 0)),
            _resident((d, N_IN), lambda b, i: (0, 0)),
        ],
        out_specs=[pl.BlockSpec((None, tm, N_IN), lambda b, i: (b, i, 0)), g_spec, g_spec],
        out_shape=[jax.ShapeDtypeStruct((bsz, t_len, N_IN), BF16), g_shape, g_shape],
        compiler_params=_params(("parallel", "parallel")),
        name="in_projection",
    )(x, mod_l, gain, lb, gna, gnb, cos_t, sin_t, w_in_l)


def _hgrn_diag_lhs(q_bf, k, gc, x_s, row0, l_a):
    pack = 2 * SUB
    for m in range(l_a // pack):
        q_m = q_bf[m * pack:(m + 1) * pack]
        pieces = []
        for s in range(SUB):
            halves = []
            for lo in (m * pack, m * pack + SUB):
                decay = jnp.exp2(jnp.minimum(gc[lo:lo + SUB] - gc[lo + s:lo + s + 1], 0.0))
                halves.append(decay * k[lo + s:lo + s + 1])
            pieces.append(jnp.concatenate(halves, axis=0).astype(BF16) * q_m)
        x_s[row0 + m * pack:row0 + (m + 1) * pack, :] = jnp.concatenate(pieces, axis=1)


def _hgrn_off_diag(q, k, gc, l_a):
    q_parts, k_parts = [], []
    for j in range(l_a // SUB - 1):
        lo, hi = j * SUB, (j + 1) * SUB
        g_ref = gc[hi - 1:hi]
        q_hat = q[hi:] * jnp.exp2(gc[hi:] - g_ref)
        k_bar = k[lo:hi] * jnp.exp2(g_ref - gc[lo:hi])
        q_parts.append(jnp.concatenate([jnp.zeros((hi, DK_A), F32), q_hat], axis=0))
        k_rows = [k_bar, jnp.zeros((l_a - hi, DK_A), F32)]
        if lo:
            k_rows.insert(0, jnp.zeros((lo, DK_A), F32))
        k_parts.append(jnp.concatenate(k_rows, axis=0))
    return _dot_nt(jnp.concatenate(q_parts, axis=1).astype(BF16),
                   jnp.concatenate(k_parts, axis=1).astype(BF16))


def _recurrence_kernel(proj_ref, ghi_ref, glo_ref, bsel_ref, sa0_ref, sb0_ref,
                       sa_all_ref, sb_all_ref, oa_ref, ob_ref, sa_out_ref, sb_out_ref,
                       sa_s, sb_s, x_s, a_sc, qt_sc, pk_sc, gc_s, *, tb, l_a, l_b):
    del sa_all_ref, sb_all_ref
    t = pl.program_id(1)

    @pl.when(t == 0)
    def _():
        sa_s[...] = sa0_ref[...]
        sb_s[...] = sb0_ref[...]

    ri = lax.broadcasted_iota(jnp.int32, (l_a, l_a), 0)
    ci = lax.broadcasted_iota(jnp.int32, (l_a, l_a), 1)
    tril = (ri >= ci).astype(BF16)
    diag_mask = ((ri // SUB) == (ci // SUB)) & (ri >= ci)

    def chunk_rows(c):
        return pl.ds(c * l_a, l_a)

    def stage1_prep(c, slot):
        rows = chunk_rows(c)
        gc_s[slot] = _dot(tril, ghi_ref[rows, :]) + _dot(tril, glo_ref[rows, :])
        return rows

    def stage1_head(rows, h, slot):
        gc = gc_s[slot, :, h * DK_A:(h + 1) * DK_A]
        q_bf = proj_ref[rows, OFF_QA + h * DK_A:OFF_QA + (h + 1) * DK_A]
        q = q_bf.astype(F32)
        k = proj_ref[rows, OFF_KA + h * DK_A:OFF_KA + (h + 1) * DK_A].astype(F32)
        _hgrn_diag_lhs(q_bf, k, gc, x_s, h * l_a, l_a)
        a = _dot(x_s[h * l_a:(h + 1) * l_a, :], bsel_ref[...])
        off = _hgrn_off_diag(q, k, gc, l_a) if l_a > SUB else 0.0
        a_sc[slot, h] = jnp.where(diag_mask, a, off).astype(BF16)
        qt_sc[slot, h] = (q * jnp.exp2(gc)).astype(BF16)
        g_end = gc[l_a - 1:l_a]
        k_tail = k * jnp.exp2(g_end - gc)
        packed = jnp.concatenate(
            [k_tail, jnp.broadcast_to(jnp.exp2(g_end), (DK_A - l_a, DK_A))], axis=0)
        pk_sc[slot, h] = packed.T

    def stage2_head(c, h, slot):
        rows = chunk_rows(c)
        v = proj_ref[rows, OFF_IA + h * DV_A:OFF_IA + (h + 1) * DV_A]
        og = proj_ref[rows, OFF_OGA + h * DV_A:OFF_OGA + (h + 1) * DV_A]
        s_prev = sa_s[h]
        o = _dot(a_sc[slot, h], v) + _dot(qt_sc[slot, h], s_prev.astype(BF16))
        packed_t = pk_sc[slot, h]
        sa_s[h] = (s_prev * packed_t[:, l_a:l_a + 1]
                   + _dot(packed_t[:, :l_a].astype(BF16), v))
        oa_ref[rows, h * DV_A:(h + 1) * DV_A] = _unit_rms(o).astype(BF16) * og

    ti = lax.broadcasted_iota(jnp.int32, (l_b, l_b), 0)
    si = lax.broadcasted_iota(jnp.int32, (l_b, l_b), 1)
    rel = (ti - si).astype(F32)
    tcol = lax.broadcasted_iota(jnp.int32, (l_b, 1), 0).astype(F32)
    decays = [jnp.where(ti >= si, jnp.exp2(jnp.minimum(rel * lg, 0.0)), 0.0) for lg in LOG2_GAMMA]
    q_scale = [jnp.exp2((tcol + 1.0) * lg) for lg in LOG2_GAMMA]
    k_scale = [jnp.exp2((l_b - 1.0 - tcol) * lg) for lg in LOG2_GAMMA]

    def retention_head(c, h):
        rows = pl.ds(c * l_b, l_b)
        qb = proj_ref[rows, OFF_QB + h * DK_B:OFF_QB + (h + 1) * DK_B]
        kb = proj_ref[rows, OFF_KB + h * DK_B:OFF_KB + (h + 1) * DK_B]
        v = proj_ref[rows, OFF_VB + h * DV_B:OFF_VB + (h + 1) * DV_B]
        sg = proj_ref[rows, OFF_SGB + h * DV_B:OFF_SGB + (h + 1) * DV_B]
        scores = _dot_nt(qb, kb) * decays[h]
        s_prev = sb_s[h]
        o = (_dot(scores.astype(BF16), v)
             + _dot((qb.astype(F32) * q_scale[h]).astype(BF16), s_prev.astype(BF16)))
        k_tail = (kb.astype(F32) * k_scale[h]).astype(BF16)
        sb_s[h] = s_prev * (2.0 ** (l_b * LOG2_GAMMA[h])) + _dot_tn(k_tail, v)
        ob_ref[rows, h * DV_B:(h + 1) * DV_B] = _unit_rms(o).astype(BF16) * sg

    n_a = tb // l_a
    prep = stage1_prep(0, 0)
    for h in range(H_A):
        stage1_head(prep, h, 0)
    if n_a == 1:
        for h in range(H_A):
            stage2_head(0, h, 0)
            retention_head(0, h)
    else:
        for s in range(n_a // 2):
            prep = stage1_prep(2 * s + 1, 1)
            for h in range(H_A):
                stage1_head(prep, h, 1)
                stage2_head(2 * s, h, 0)
            prep = stage1_prep(2 * s + 2, 0) if 2 * s + 2 < n_a else None
            for h in range(H_A):
                if prep is not None:
                    stage1_head(prep, h, 0)
                stage2_head(2 * s + 1, h, 1)
                retention_head(s, h)

    @pl.when(t == pl.num_programs(1) - 1)
    def _():
        sa_out_ref[...] = sa_s[...]
        sb_out_ref[...] = sb_s[...]


def _diag_selector(l_a):
    s_of_row = jnp.arange(SUB * DK_A, dtype=jnp.int32) // DK_A
    col = jnp.arange(l_a, dtype=jnp.int32)
    return (s_of_row[:, None] == (col[None, :] % SUB)).astype(BF16)


def _recurrences(proj, g_hi, g_lo, s_a0, s_b0, sa_all, sb_all, layer, tb):
    bsz, t_len, _ = proj.shape
    l_a = min(64, t_len)
    l_b = min(128, t_len)
    state_spec = pl.BlockSpec((None, H_A, DK_A, DV_A), lambda b, t: (b, 0, 0, 0))
    state_out_spec = pl.BlockSpec((None, None, H_A, DK_A, DV_A), lambda b, t: (layer, b, 0, 0, 0))
    hbm = pl.BlockSpec(memory_space=pl.ANY)
    act_spec = pl.BlockSpec((None, tb, D_MODEL), lambda b, t: (b, t, 0))
    g_spec = pl.BlockSpec((None, tb, H_A * DK_A), lambda b, t: (b, t, 0))
    return pl.pallas_call(
        functools.partial(_recurrence_kernel, tb=tb, l_a=l_a, l_b=l_b),
        grid=(bsz, t_len // tb),
        in_specs=[
            pl.BlockSpec((None, tb, REC_COLS), lambda b, t: (b, t, 0)),
            g_spec, g_spec,
            pl.BlockSpec((SUB * DK_A, l_a), lambda b, t: (0, 0)),
            state_spec, state_spec, hbm, hbm,
        ],
        out_specs=[act_spec, act_spec, state_out_spec, state_out_spec],
        out_shape=[
            jax.ShapeDtypeStruct((bsz, t_len, D_MODEL), BF16),
            jax.ShapeDtypeStruct((bsz, t_len, D_MODEL), BF16),
            jax.ShapeDtypeStruct(sa_all.shape, F32),
            jax.ShapeDtypeStruct(sb_all.shape, F32),
        ],
        input_output_aliases={6: 2, 7: 3},
        scratch_shapes=[pltpu.VMEM((H_A, DK_A, DV_A), F32), pltpu.VMEM((H_B, DK_B, DV_B), F32),
                        pltpu.VMEM((H_A * l_a, SUB * DK_A), BF16),
                        pltpu.VMEM((2, H_A, l_a, l_a), BF16),
                        pltpu.VMEM((2, H_A, l_a, DK_A), BF16),
                        pltpu.VMEM((2, H_A, DK_A, DK_A), F32),
                        pltpu.VMEM((2, l_a, H_A * DK_A), F32)],
        compiler_params=_params(("parallel", "arbitrary")),
        name="recurrences",
    )(proj, g_hi, g_lo, _diag_selector(l_a), s_a0, s_b0, sa_all, sb_all)


def _merge_kernel(*refs, with_router):
    if with_router:
        (oa_ref, ob_ref, mg_ref, x_ref, mod_ref, gains_ref, wba_ref, wbb_ref, wo_ref,
         wr_ref, br_ref, lt_ref, xo_ref, h2_ref, route_ref, cnt_ref, run_s) = refs
    else:
        (oa_ref, ob_ref, mg_ref, x_ref, mod_ref, gains_ref, wba_ref, wbb_ref, wo_ref,
         xo_ref, h2_ref) = refs
    d = D_MODEL
    tm = x_ref.shape[0]
    halves = [pl.ds(0, tm // 2), pl.ds(tm // 2, tm // 2)] if tm >= 256 else [pl.ds(0, tm)]
    merged = []
    for rows in halves:
        y_a = _dot(oa_ref[rows, :], wba_ref[...])
        y_b = _dot(ob_ref[rows, :], wbb_ref[...])
        merged.append((mg_ref[rows, :d].astype(F32) * y_a
                       + mg_ref[rows, d:].astype(F32) * y_b).astype(BF16))
    m_outs = [_dot(m, wo_ref[...]) for m in merged]
    if with_router:
        w_hi, w_lo = _split_bf16(wr_ref[...])
    logit_parts = []
    for rows, m_out in zip(halves, m_outs):
        x = x_ref[rows, :] + mod_ref[2] * _rms(m_out, gains_ref[1:2])
        xo_ref[rows, :] = x
        h2 = _rms(x, gains_ref[2:3]) * (1.0 + mod_ref[4]) + mod_ref[3]
        if not with_router:
            h2_ref[rows, :] = h2.astype(BF16)
            continue
        h2_ref[rows, :] = h2
        h_hi, h_lo = _split_bf16(h2)
        logit_parts.append(_dot_nt(w_hi, h_hi) + _dot_nt(w_lo, h_hi) + _dot_nt(w_hi, h_lo))
    if not with_router:
        return

    @pl.when((pl.program_id(0) == 0) & (pl.program_id(1) == 0))
    def _():
        run_s[...] = jnp.zeros_like(run_s)

    logits = jnp.concatenate(logit_parts, axis=1) + br_ref[...]
    exp_id = lax.broadcasted_iota(jnp.int32, logits.shape, 0)
    m1 = jnp.max(logits, axis=0, keepdims=True)
    i1 = jnp.min(jnp.where(logits == m1, exp_id, N_EXPERTS), axis=0, keepdims=True)
    rest = jnp.where(exp_id == i1, -jnp.inf, logits)
    m2 = jnp.max(rest, axis=0, keepdims=True)
    i2 = jnp.min(jnp.where(rest == m2, exp_id, N_EXPERTS), axis=0, keepdims=True)
    e2 = jnp.exp(m2 - m1)
    w1 = 1.0 / (1.0 + e2)
    w2 = e2 / (1.0 + e2)
    chosen = ((exp_id == i1) | (exp_id == i2)).astype(F32)
    before = run_s[...] + _dot(chosen.astype(BF16), lt_ref[...])
    r1 = jnp.sum(jnp.where(exp_id == i1, before, 0.0), axis=0, keepdims=True)
    r2 = jnp.sum(jnp.where(exp_id == i2, before, 0.0), axis=0, keepdims=True)
    run_s[...] = run_s[...] + jnp.sum(chosen, axis=1, keepdims=True)
    cnt_ref[...] = run_s[...]
    route = jnp.zeros(logits.shape, F32)
    for j, val in enumerate((i1.astype(F32), i2.astype(F32), r1, r2, w1, w2)):
        route = jnp.where(exp_id == j, val, route)
    route_ref[...] = route


def _merge(o_a, o_b, proj, x, mod_l, gains_l, wba, wbb, wo, router, tm):
    bsz, t_len, d = x.shape
    with_router = router is not None
    tok = pl.BlockSpec((None, tm, d), lambda b, i: (b, i, 0))
    in_specs = [
        tok, tok,
        pl.BlockSpec((None, tm, 2 * d), lambda b, i: (b, i, OFF_MGA // (2 * d))),
        tok,
        pl.BlockSpec((N_MOD, None, 1, d), lambda b, i: (0, b, 0, 0)),
        pl.BlockSpec((4, d), lambda b, i: (0, 0)),
        _resident((d, d), lambda b, i: (0, 0)),
        _resident((d, d), lambda b, i: (0, 0)),
        _resident((d, d), lambda b, i: (0, 0)),
    ]
    args = [o_a, o_b, proj, x, mod_l, gains_l, wba, wbb, wo]
    if not with_router:
        return pl.pallas_call(
            functools.partial(_merge_kernel, with_router=False),
            grid=(bsz, t_len // tm),
            in_specs=in_specs,
            out_specs=[tok, tok],
            out_shape=[jax.ShapeDtypeStruct((bsz, t_len, d), F32),
                       jax.ShapeDtypeStruct((bsz, t_len, d), BF16)],
            compiler_params=_params(("parallel", "parallel")),
            name="merge",
        )(*args)
    w_r, b_r = router
    earlier = jnp.triu(jnp.ones((tm, tm), BF16), 1)
    in_specs += [pl.BlockSpec((N_EXPERTS, d), lambda b, i: (0, 0)),
                 pl.BlockSpec((N_EXPERTS, 1), lambda b, i: (0, 0)),
                 pl.BlockSpec((tm, tm), lambda b, i: (0, 0))]
    args += [w_r.T, b_r.reshape(N_EXPERTS, 1), earlier]
    nt = t_len // tm
    return pl.pallas_call(
        functools.partial(_merge_kernel, with_router=True),
        grid=(bsz, nt),
        in_specs=in_specs,
        out_specs=[tok, tok,
                   pl.BlockSpec((None, None, N_EXPERTS, tm), lambda b, i: (b, i, 0, 0)),
                   pl.BlockSpec((N_EXPERTS, 1), lambda b, i: (0, 0))],
        out_shape=[jax.ShapeDtypeStruct((bsz, t_len, d), F32),
                   jax.ShapeDtypeStruct((bsz, t_len, d), F32),
                   jax.ShapeDtypeStruct((bsz, nt, N_EXPERTS, tm), F32),
                   jax.ShapeDtypeStruct((N_EXPERTS, 1), F32)],
        scratch_shapes=[pltpu.VMEM((N_EXPERTS, 1), F32)],
        compiler_params=_params(("arbitrary", "arbitrary")),
        name="merge_router",
    )(*args)


def _swiglu_hidden(hb, wg_ref, wu_ref, a_s):
    for c in range(FFN_DIM // MXU_N):
        sl = slice(c * MXU_N, (c + 1) * MXU_N)
        gate = _dot(hb, wg_ref[:, sl])
        up = _dot(hb, wu_ref[:, sl])
        a_s[:, sl] = (gate * _sigmoid(gate) * up).astype(BF16)


def _ffn_dense_kernel(h_ref, wg_ref, wu_ref, wd_ref, x_ref, mod_ref, gain_ref, o_ref, a_s):
    _swiglu_hidden(h_ref[...], wg_ref, wu_ref, a_s)
    y = _dot(a_s[...], wd_ref[...])
    o_ref[...] = x_ref[...] + mod_ref[0] * _rms(y, gain_ref[...])


def _ffn_dense(h2, wg, wu, wd, x, mod_l, gain, tm):
    bsz, t_len, d = x.shape
    tok = pl.BlockSpec((None, tm, d), lambda b, i: (b, i, 0))
    return pl.pallas_call(
        _ffn_dense_kernel,
        grid=(bsz, t_len // tm),
        in_specs=[
            tok,
            _resident((d, FFN_DIM), lambda b, i: (0, 0)),
            _resident((d, FFN_DIM), lambda b, i: (0, 0)),
            _resident((FFN_DIM, d), lambda b, i: (0, 0)),
            tok,
            pl.BlockSpec((1, None, 1, d), lambda b, i: (N_MOD - 1, b, 0, 0)),
            pl.BlockSpec((1, d), lambda b, i: (0, 0)),
        ],
        out_specs=tok,
        out_shape=jax.ShapeDtypeStruct((bsz, t_len, d), F32),
        scratch_shapes=[pltpu.VMEM((tm, FFN_DIM), BF16)],
        compiler_params=_params(("parallel", "parallel")),
        name="ffn_dense",
    )(h2, wg, wu, wd, x, mod_l, gain)


def _ffn_grouped_kernel(te_ref, nu_ref, x_ref, wg_ref, wu_ref, wd_ref, y_ref, a_s):
    del te_ref

    @pl.when(pl.program_id(0) < nu_ref[0])
    def _():
        tmg = x_ref.shape[0]
        x = x_ref[...].reshape(tmg, D_MODEL)
        _swiglu_hidden(x.astype(BF16), wg_ref, wu_ref, a_s)
        y_ref[...] = _dot(a_s[...], wd_ref[...]).reshape(y_ref.shape)


def _ffn_grouped(x_sorted, tile_expert, n_used, wg, wu, wd, tmg):
    n_slots = x_sorted.shape[0]
    d = D_MODEL
    n_tiles = n_slots // tmg

    def row_map(i, te, nu):
        return (jnp.minimum(i, nu[0] - 1), 0, 0)

    def w_map(i, te, nu):
        return (te[i], 0, 0)

    return pl.pallas_call(
        _ffn_grouped_kernel,
        grid_spec=pltpu.PrefetchScalarGridSpec(
            num_scalar_prefetch=2,
            grid=(n_tiles,),
            in_specs=[
                pl.BlockSpec((tmg,) + ROW_TILE, row_map),
                pl.BlockSpec((None, d, FFN_DIM), w_map),
                pl.BlockSpec((None, d, FFN_DIM), w_map),
                pl.BlockSpec((None, FFN_DIM, d), w_map),
            ],
            out_specs=pl.BlockSpec((tmg,) + ROW_TILE, row_map),
            scratch_shapes=[pltpu.VMEM((tmg, FFN_DIM), BF16)],
        ),
        out_shape=jax.ShapeDtypeStruct((n_slots,) + ROW_TILE, F32),
        compiler_params=_params(("arbitrary",)),
        name="ffn_grouped",
    )(tile_expert, n_used, x_sorted, wg, wu, wd)


def _row_copy(src_ref, dst_ref, sem):
    return pltpu.make_async_copy(src_ref, dst_ref, sem)


def _start_rows(copies, n_rows):
    def issue(r, carry):
        for i, cp in enumerate(copies(r)):
            cp.start(priority=i % 2)
        return carry

    lax.fori_loop(0, n_rows, issue, 0, unroll=8)


def _wait_rows(copies, n_rows):
    def drain(r, carry):
        for cp in copies(r):
            cp.wait()
        return carry

    lax.fori_loop(0, n_rows, drain, 0, unroll=8)


def _scatter_kernel(s1_ref, s2_ref, p1_ref, p2_ref, ztile_ref, h_ref, xs_ref,
                    rows_s, zero_s, sems, zsem):
    tm = h_ref.shape[0]
    tmg = zero_s.shape[0]
    i = pl.program_id(0)
    par = i % 2

    @pl.when(i == 0)
    def _():
        zero_s[...] = jnp.zeros_like(zero_s)

        def zero_copy(e):
            return pltpu.make_async_copy(zero_s, xs_ref.at[pl.ds(ztile_ref[e], tmg)], zsem)

        for e in range(N_EXPERTS):
            pl.when(ztile_ref[e] >= 0)(lambda e=e: zero_copy(e).start())
        for e in range(N_EXPERTS):
            pl.when(ztile_ref[e] >= 0)(lambda e=e: zero_copy(e).wait())

    def row_copies(slot_a, slot_b, p):
        def copies(r):
            src = rows_s.at[p, pl.ds(r, 1)]
            return (_row_copy(src, xs_ref.at[pl.ds(slot_a[0, r], 1)], sems.at[p]),
                    _row_copy(src, xs_ref.at[pl.ds(slot_b[0, r], 1)], sems.at[p]))
        return copies

    rows_s[par] = h_ref[...].reshape(rows_s.shape[1:])
    _start_rows(row_copies(s1_ref, s2_ref, par), tm)
    pl.when(i > 0)(lambda: _wait_rows(row_copies(p1_ref, p2_ref, 1 - par), tm))
    pl.when(i == pl.num_programs(0) - 1)(lambda: _wait_rows(row_copies(s1_ref, s2_ref, par), tm))


def _scatter_rows(h2, slot1, slot2, zero_tile_row, n_slots, tm, tmg):
    bsz, t_len, d = h2.shape
    n = bsz * t_len // tm

    def slot_spec(shift):
        return pl.BlockSpec((None, 1, tm), lambda i: (jnp.clip(i + shift, 0, n - 1), 0, 0),
                            memory_space=pltpu.SMEM)

    return pl.pallas_call(
        _scatter_kernel,
        grid=(n,),
        in_specs=[slot_spec(0), slot_spec(0), slot_spec(-1), slot_spec(-1),
                  pl.BlockSpec(memory_space=pltpu.SMEM),
                  pl.BlockSpec((tm, d), lambda i: (i, 0))],
        out_specs=pl.BlockSpec(memory_space=pl.ANY),
        out_shape=jax.ShapeDtypeStruct((n_slots,) + ROW_TILE, F32),
        scratch_shapes=[pltpu.VMEM((2, tm) + ROW_TILE, F32), pltpu.VMEM((tmg,) + ROW_TILE, F32),
                        pltpu.SemaphoreType.DMA((2,)), pltpu.SemaphoreType.DMA],
        compiler_params=_params(("arbitrary",)),
        name="scatter_rows",
    )(slot1, slot2, slot1, slot2, zero_tile_row, h2.reshape(bsz * t_len, d))


def _combine_kernel(s1_ref, s2_ref, n1_ref, n2_ref, route_ref, x_ref, mod_ref, gain_ref, y_ref,
                    o_ref, buf1, buf2, sems):
    tm, d = x_ref.shape
    i = pl.program_id(0)
    par = i % 2

    def row_copies(slot_a, slot_b, p):
        def copies(r):
            return (_row_copy(y_ref.at[pl.ds(slot_a[0, r], 1)], buf1.at[p, pl.ds(r, 1)], sems.at[p]),
                    _row_copy(y_ref.at[pl.ds(slot_b[0, r], 1)], buf2.at[p, pl.ds(r, 1)], sems.at[p]))
        return copies

    pl.when(i == 0)(lambda: _start_rows(row_copies(s1_ref, s2_ref, par), tm))
    pl.when(i + 1 < pl.num_programs(0))(
        lambda: _start_rows(row_copies(n1_ref, n2_ref, 1 - par), tm))
    _wait_rows(row_copies(s1_ref, s2_ref, par), tm)
    weights = route_ref[...].T
    f_out = (weights[:, 4:5] * buf1[par].reshape(tm, d)
             + weights[:, 5:6] * buf2[par].reshape(tm, d))
    o_ref[...] = x_ref[...] + mod_ref[0] * _rms(f_out, gain_ref[...])


def _combine_rows(y_sorted, slot1, slot2, route, x, mod_l, gain, tm):
    bsz, t_len, d = x.shape
    nt = t_len // tm
    n = bsz * nt

    def slot_spec(shift):
        return pl.BlockSpec((None, 1, tm), lambda i: (jnp.minimum(i + shift, n - 1), 0, 0),
                            memory_space=pltpu.SMEM)

    tok = pl.BlockSpec((None, tm, d), lambda i: (i // nt, i % nt, 0))
    return pl.pallas_call(
        _combine_kernel,
        grid=(n,),
        in_specs=[slot_spec(0), slot_spec(0), slot_spec(1), slot_spec(1),
                  pl.BlockSpec((None, None, N_EXPERTS, tm), lambda i: (i // nt, i % nt, 0, 0)),
                  tok,
                  pl.BlockSpec((1, None, 1, d), lambda i: (N_MOD - 1, i // nt, 0, 0)),
                  pl.BlockSpec((1, d), lambda i: (0, 0)),
                  pl.BlockSpec(memory_space=pl.ANY)],
        out_specs=tok,
        out_shape=jax.ShapeDtypeStruct((bsz, t_len, d), F32),
        scratch_shapes=[pltpu.VMEM((2, tm) + ROW_TILE, F32), pltpu.VMEM((2, tm) + ROW_TILE, F32),
                        pltpu.SemaphoreType.DMA((2,))],
        compiler_params=_params(("arbitrary",)),
        name="combine_rows",
    )(slot1, slot2, slot1, slot2, route, x, mod_l, gain, y_sorted)


def _moe_ffn(h2, route, counts, wg, wu, wd, first_expert, x, mod_l, gain, tm, tmg):
    bsz, t_len, d = x.shape
    n_tok = bsz * t_len
    n_tiles = (2 * n_tok) // tmg + N_EXPERTS
    e1, e2, r1, r2 = (route[:, :, j, :].astype(jnp.int32).reshape(n_tok) for j in range(4))
    cnt = counts[:, 0].astype(jnp.int32)
    tiles_per = (cnt + tmg - 1) // tmg
    tile_end = jnp.cumsum(tiles_per)
    offs = (tile_end - tiles_per) * tmg
    n_used = tile_end[-1:]
    tile_ids = jnp.arange(n_tiles, dtype=jnp.int32)
    tile_expert = jnp.sum(tile_ids[:, None] >= tile_end[None, :], axis=1).astype(jnp.int32)
    last_expert = jnp.sum(n_used - 1 >= tile_end).astype(jnp.int32)
    tile_expert = jnp.where(tile_ids < n_used, tile_expert, last_expert)
    zero_tile_row = jnp.where(cnt > 0, (tile_end - 1) * tmg, -1).astype(jnp.int32)
    nt = t_len // tm
    slot1 = (offs[e1] + r1).reshape(bsz * nt, 1, tm)
    slot2 = (offs[e2] + r2).reshape(bsz * nt, 1, tm)
    x_sorted = _scatter_rows(h2, slot1, slot2, zero_tile_row, n_tiles * tmg, tm, tmg)
    y_sorted = _ffn_grouped(x_sorted, tile_expert + first_expert, n_used, wg, wu, wd, tmg)
    return _combine_rows(y_sorted, slot1, slot2, route, x, mod_l, gain, tm)


def _rope_tables(pos):
    half = DK_B // 2
    inv = ROPE_BASE ** (-jnp.arange(half, dtype=F32) / half)
    ang = pos.astype(F32)[:, None] * inv[None, :]
    cos, sin = jnp.cos(ang), jnp.sin(ang)
    return jnp.concatenate([cos, cos], axis=-1), jnp.concatenate([-sin, sin], axis=-1)


def _trunk(x, mod, pos, s_a_in, s_b_in, lower_bounds, wts, tm, tb, tmg):
    depth = mod.shape[0]
    bsz = x.shape[0]
    cos_t, sin_t = _rope_tables(pos)
    sa_all = jnp.zeros((depth, bsz, H_A, DK_A, DV_A), F32)
    sb_all = jnp.zeros((depth, bsz, H_B, DK_B, DV_B), F32)
    for l in range(depth):
        mod_l = mod[l]
        gains_l = wts["norm_gains"][l]
        if s_a_in is None:
            s_a = jnp.zeros((bsz, H_A, DK_A, DV_A), F32)
            s_b = jnp.zeros((bsz, H_B, DK_B, DV_B), F32)
        else:
            s_a, s_b = s_a_in[l], s_b_in[l]
        proj, g_hi, g_lo = _in_projection(
            x, mod_l, gains_l[0:1], lower_bounds[l:l + 1],
            wts["g_norm_a"][l].reshape(1, H_A * DV_A), wts["g_norm_b"][l].reshape(1, H_B * DV_B),
            cos_t, sin_t, wts["w_in"][l], tm)
        o_a, o_b, sa_all, sb_all = _recurrences(proj, g_hi, g_lo, s_a, s_b, sa_all, sb_all, l, tb)
        j = l // 2
        router = None if l % 2 == 0 else (wts["w_router"][j], wts["b_router"][j])
        outs = _merge(o_a, o_b, proj, x, mod_l, gains_l, wts["w_branch_a"][l],
                      wts["w_branch_b"][l], wts["w_out"][l], router, tm)
        if l % 2 == 0:
            x, h2 = outs
            x = _ffn_dense(h2, wts["w_gate_dense"][j], wts["w_up_dense"][j],
                           wts["w_down_dense"][j], x, mod_l, gains_l[3:4], tm)
        else:
            x, h2, route, counts = outs
            x = _moe_ffn(h2, route, counts, wts["w_gate_exp"], wts["w_up_exp"],
                         wts["w_down_exp"], j * N_EXPERTS, x, mod_l, gains_l[3:4], tm, tmg)
    return x, sa_all, sb_all


def kernel(x_prompt, x_sample, state_hgrn, state_ret, c_prompt, c_sample, w_mod, b_mod, norm_gains,
           w_in, lb_logits, g_norm_a, g_norm_b, w_branch_a, w_branch_b, w_out, w_gate_dense,
           w_up_dense, w_down_dense, w_router, b_router, w_gate_exp, w_up_exp, w_down_exp):
    n_prompt = x_prompt.shape[0]
    past_len = 1024
    wts = dict(
        norm_gains=norm_gains, g_norm_a=g_norm_a, g_norm_b=g_norm_b,
        w_router=w_router, b_router=b_router,
        w_in=w_in.astype(BF16), w_branch_a=w_branch_a.astype(BF16),
        w_branch_b=w_branch_b.astype(BF16), w_out=w_out.astype(BF16),
        w_gate_dense=w_gate_dense.astype(BF16), w_up_dense=w_up_dense.astype(BF16),
        w_down_dense=w_down_dense.astype(BF16),
        w_gate_exp=w_gate_exp.astype(BF16).reshape((-1,) + w_gate_exp.shape[2:]),
        w_up_exp=w_up_exp.astype(BF16).reshape((-1,) + w_up_exp.shape[2:]),
        w_down_exp=w_down_exp.astype(BF16).reshape((-1,) + w_down_exp.shape[2:]))
    lb_p = jax.nn.softmax(lb_logits.astype(F32), axis=0)
    lower_bounds = jnp.clip(jnp.cumsum(lb_p, axis=0) - lb_p[:1], 0.0, 1.0)
    c_all = jnp.concatenate([c_prompt, c_sample], axis=0)
    mod = _modulation(c_all, w_mod, b_mod)
    mod = mod.reshape(mod.shape[:3] + (1, D_MODEL))
    pos_prompt = jnp.arange(x_prompt.shape[1], dtype=jnp.int32)
    pos_sample = past_len + jnp.arange(x_sample.shape[1], dtype=jnp.int32)
    t_p, t_s = x_prompt.shape[1], x_sample.shape[1]
    y_p, hg_p, rt_p = _trunk(x_prompt, mod[:, :, :n_prompt], pos_prompt, None, None,
                             lower_bounds, wts, tm=min(512, t_p), tb=min(1024, t_p), tmg=512)
    y_s, hg_s, rt_s = _trunk(x_sample, mod[:, :, n_prompt:], pos_sample, state_hgrn, state_ret,
                             lower_bounds, wts, tm=min(512, t_s), tb=min(512, t_s), tmg=64)
    return (y_p, y_s, hg_p, rt_p, hg_s, rt_s)
```

```python
import functools
import math

import jax
import jax.numpy as jnp
from jax import lax
from jax.experimental import pallas as pl
from jax.experimental.pallas import tpu as pltpu

F32 = jnp.float32
BF16 = jnp.bfloat16

D_MODEL = 1024
H_A, DK_A, DV_A = 4, 128, 256
H_B, DK_B, DV_B = 4, 128, 256
ROPE_BASE = 10000.0
FFN_DIM = 2816
N_EXPERTS = 8
RMS_EPS = 1e-6
F_FLOOR = 1e-20
N_MOD = 6
N_IN = 8 * D_MODEL
OFF_QA, OFF_KA, OFF_IA, OFF_OGA = 0, 512, 1024, 2048
OFF_QB, OFF_KB, OFF_VB, OFF_SGB = 3072, 3584, 4096, 5120
OFF_MGA, OFF_MGB = 6144, 7168
REC_COLS = 6144
SUB = 8
LOG2_GAMMA = tuple(math.log2(1.0 - 2.0 ** (-5.0 - h)) for h in range(H_B))
VMEM_LIMIT = 56 * 1024 * 1024
MXU_N = 256
ROW_TILE = (8, 128)


def _params(sem):
    return pltpu.CompilerParams(dimension_semantics=sem, vmem_limit_bytes=VMEM_LIMIT)


def _resident(shape, index_map):
    return pl.BlockSpec(shape, index_map, pipeline_mode=pl.Buffered(1))


def _rms(x, gain):
    return x * lax.rsqrt(jnp.mean(x * x, axis=-1, keepdims=True) + RMS_EPS) * gain


def _unit_rms(x):
    return x * lax.rsqrt(jnp.mean(x * x, axis=-1, keepdims=True) + RMS_EPS)


def _sigmoid(x):
    return 1.0 / (1.0 + jnp.exp(-x))


def _dot(a, b):
    return jnp.dot(a, b, preferred_element_type=F32)


def _dot_nt(a, b):
    return lax.dot_general(a, b, (((1,), (1,)), ((), ())), preferred_element_type=F32)


def _dot_tn(a, b):
    return lax.dot_general(a, b, (((0,), (0,)), ((), ())), preferred_element_type=F32)


def _split_bf16(x):
    hi = x.astype(BF16)
    return hi, (x - hi.astype(F32)).astype(BF16)


def _mod_kernel(c_ref, w_ref, b_ref, o_ref):
    c = c_ref[...]
    c_act = (c * _sigmoid(c)).astype(BF16)
    o_ref[...] = _dot(c_act, w_ref[...].astype(BF16)) + b_ref[...]


def _modulation(c_all, w_mod, b_mod):
    depth = w_mod.shape[0]
    n_seq = c_all.shape[0]
    d = D_MODEL
    return pl.pallas_call(
        _mod_kernel,
        grid=(depth, N_MOD),
        in_specs=[
            pl.BlockSpec((n_seq, d), lambda l, j: (0, 0)),
            pl.BlockSpec((None, d, d), lambda l, j: (l, 0, j)),
            pl.BlockSpec((None, None, 1, d), lambda l, j: (l, j, 0, 0)),
        ],
        out_specs=pl.BlockSpec((None, None, n_seq, d), lambda l, j: (l, j, 0, 0)),
        out_shape=jax.ShapeDtypeStruct((depth, N_MOD, n_seq, d), F32),
        compiler_params=_params(("parallel", "parallel")),
        name="modulation",
    )(c_all, w_mod, b_mod.reshape(depth, N_MOD, 1, d))


def _inproj_kernel(x_ref, mod_ref, gain_ref, lb_ref, gna_ref, gnb_ref, cos_ref, sin_ref, w_ref,
                   o_ref, ghi_ref, glo_ref):
    d = D_MODEL
    h = _rms(x_ref[...], gain_ref[...] * (1.0 + mod_ref[1])) + mod_ref[0]
    hb = h.astype(BF16)

    def seg(cols):
        return _dot(hb, w_ref[:, cols[0]:cols[0] + cols[1]])

    def hgrn_q(q):
        o_ref[:, OFF_QA:OFF_QA + 512] = (q * _sigmoid(q)).astype(BF16)

    def hgrn_f(f_raw):
        lb = lb_ref[...]
        open_part = (1.0 - lb) * _sigmoid(f_raw)
        g2 = jnp.log2(jnp.maximum(lb + open_part, F_FLOOR))
        ghi, glo = _split_bf16(g2)
        ghi_ref[...] = ghi
        glo_ref[...] = glo
        o_ref[:, OFF_KA:OFF_KA + 512] = ((1.0 - lb) - open_part).astype(BF16)

    def retention_qk(y):
        cos, sin = cos_ref[...], sin_ref[...]
        for hh in range(H_B):
            xq = y[:, hh * DK_B:(hh + 1) * DK_B]
            xk = y[:, 512 + hh * DK_B:512 + (hh + 1) * DK_B]
            rq = xq * cos + pltpu.roll(xq, DK_B // 2, axis=1) * sin
            rk = (xk * cos + pltpu.roll(xk, DK_B // 2, axis=1) * sin) * (DK_B ** -0.5)
            o_ref[:, OFF_QB + hh * DK_B:OFF_QB + (hh + 1) * DK_B] = rq.astype(BF16)
            o_ref[:, OFF_KB + hh * DK_B:OFF_KB + (hh + 1) * DK_B] = rk.astype(BF16)

    def store(off, act):
        def epilogue(y):
            o_ref[:, off:off + d] = act(y).astype(BF16)
        return epilogue

    plan = [((1 * d, d), store(OFF_IA, lambda y: y)), ((512, 512), hgrn_f),
            ((2 * d, d), store(OFF_OGA, lambda y: _sigmoid(y) * gna_ref[...])), ((0, 512), hgrn_q),
            ((3 * d, d), retention_qk),
            ((5 * d, d), store(OFF_SGB, lambda y: y * _sigmoid(y) * gnb_ref[...])),
            ((6 * d, d), store(OFF_MGA, _sigmoid)), ((7 * d, d), store(OFF_MGB, _sigmoid)),
            ((4 * d, d), store(OFF_VB, lambda y: y))]
    y = seg(plan[0][0])
    for i, (_, epilogue) in enumerate(plan):
        y_next = seg(plan[i + 1][0]) if i + 1 < len(plan) else None
        epilogue(y)
        y = y_next


def _in_projection(x, mod_l, gain, lb, gna, gnb, cos_t, sin_t, w_in_l, tm):
    bsz, t_len, d = x.shape
    g_spec = pl.BlockSpec((None, tm, H_A * DK_A), lambda b, i: (b, i, 0))
    g_shape = jax.ShapeDtypeStruct((bsz, t_len, H_A * DK_A), BF16)
    return pl.pallas_call(
        _inproj_kernel,
        grid=(bsz, t_len // tm),
        in_specs=[
            pl.BlockSpec((None, tm, d), lambda b, i: (b, i, 0)),
            pl.BlockSpec((3, None, 1, d), lambda b, i: (0, b, 0, 0)),
            pl.BlockSpec((1, d), lambda b, i: (0, 0)),
            pl.BlockSpec((1, H_A * DK_A), lambda b, i: (0, 0)),
            pl.BlockSpec((1, d), lambda b, i: (0, 0)),
            pl.BlockSpec((1, d), lambda b, i: (0, 0)),
            pl.BlockSpec((tm, DK_B), lambda b, i: (i, 0)),
            pl.BlockSpec((tm, DK_B), lambda b, i: (i, 0)),
            _resident((d, N_IN), lambda b, i: (0, 0)),
        ],
        out_specs=[pl.BlockSpec((None, tm, N_IN), lambda b, i: (b, i, 0)), g_spec, g_spec],
        out_shape=[jax.ShapeDtypeStruct((bsz, t_len, N_IN), BF16), g_shape, g_shape],
        compiler_params=_params(("parallel", "parallel")),
        name="in_projection",
    )(x, mod_l, gain, lb, gna, gnb, cos_t, sin_t, w_in_l)


def _hgrn_diag_lhs(q_bf, k, gc, x_s, row0, l_a):
    pack = 2 * SUB
    for m in range(l_a // pack):
        q_m = q_bf[m * pack:(m + 1) * pack]
        pieces = []
        for s in range(SUB):
            halves = []
            for lo in (m * pack, m * pack + SUB):
                decay = jnp.exp2(jnp.minimum(gc[lo:lo + SUB] - gc[lo + s:lo + s + 1], 0.0))
                halves.append(decay * k[lo + s:lo + s + 1])
            pieces.append(jnp.concatenate(halves, axis=0).astype(BF16) * q_m)
        x_s[row0 + m * pack:row0 + (m + 1) * pack, :] = jnp.concatenate(pieces, axis=1)


def _hgrn_off_diag(q, k, gc, l_a):
    q_parts, k_parts = [], []
    for j in range(l_a // SUB - 1):
        lo, hi = j * SUB, (j + 1) * SUB
        g_ref = gc[hi - 1:hi]
        q_hat = q[hi:] * jnp.exp2(gc[hi:] - g_ref)
        k_bar = k[lo:hi] * jnp.exp2(g_ref - gc[lo:hi])
        q_parts.append(jnp.concatenate([jnp.zeros((hi, DK_A), F32), q_hat], axis=0))
        k_rows = [k_bar, jnp.zeros((l_a - hi, DK_A), F32)]
        if lo:
            k_rows.insert(0, jnp.zeros((lo, DK_A), F32))
        k_parts.append(jnp.concatenate(k_rows, axis=0))
    return _dot_nt(jnp.concatenate(q_parts, axis=1).astype(BF16),
                   jnp.concatenate(k_parts, axis=1).astype(BF16))


def _recurrence_kernel(proj_ref, ghi_ref, glo_ref, bsel_ref, sa0_ref, sb0_ref,
                       sa_all_ref, sb_all_ref, oa_ref, ob_ref, sa_out_ref, sb_out_ref,
                       sa_s, sb_s, x_s, a_sc, qt_sc, pk_sc, gc_s, *, tb, l_a, l_b):
    del sa_all_ref, sb_all_ref
    t = pl.program_id(1)

    @pl.when(t == 0)
    def _():
        sa_s[...] = sa0_ref[...]
        sb_s[...] = sb0_ref[...]

    ri = lax.broadcasted_iota(jnp.int32, (l_a, l_a), 0)
    ci = lax.broadcasted_iota(jnp.int32, (l_a, l_a), 1)
    tril = (ri >= ci).astype(BF16)
    diag_mask = ((ri // SUB) == (ci // SUB)) & (ri >= ci)

    def chunk_rows(c):
        return pl.ds(c * l_a, l_a)

    def stage1_prep(c, slot):
        rows = chunk_rows(c)
        gc_s[slot] = _dot(tril, ghi_ref[rows, :]) + _dot(tril, glo_ref[rows, :])
        return rows

    def stage1_head(rows, h, slot):
        gc = gc_s[slot, :, h * DK_A:(h + 1) * DK_A]
        q_bf = proj_ref[rows, OFF_QA + h * DK_A:OFF_QA + (h + 1) * DK_A]
        q = q_bf.astype(F32)
        k = proj_ref[rows, OFF_KA + h * DK_A:OFF_KA + (h + 1) * DK_A].astype(F32)
        _hgrn_diag_lhs(q_bf, k, gc, x_s, h * l_a, l_a)
        a = _dot(x_s[h * l_a:(h + 1) * l_a, :], bsel_ref[...])
        off = _hgrn_off_diag(q, k, gc, l_a) if l_a > SUB else 0.0
        a_sc[slot, h] = jnp.where(diag_mask, a, off).astype(BF16)
        qt_sc[slot, h] = (q * jnp.exp2(gc)).astype(BF16)
        g_end = gc[l_a - 1:l_a]
        k_tail = k * jnp.exp2(g_end - gc)
        packed = jnp.concatenate(
            [k_tail, jnp.broadcast_to(jnp.exp2(g_end), (DK_A - l_a, DK_A))], axis=0)
        pk_sc[slot, h] = packed.T

    def stage2_head(c, h, slot):
        rows = chunk_rows(c)
        v = proj_ref[rows, OFF_IA + h * DV_A:OFF_IA + (h + 1) * DV_A]
        og = proj_ref[rows, OFF_OGA + h * DV_A:OFF_OGA + (h + 1) * DV_A]
        s_prev = sa_s[h]
        o = _dot(a_sc[slot, h], v) + _dot(qt_sc[slot, h], s_prev.astype(BF16))
        packed_t = pk_sc[slot, h]
        sa_s[h] = (s_prev * packed_t[:, l_a:l_a + 1]
                   + _dot(packed_t[:, :l_a].astype(BF16), v))
        oa_ref[rows, h * DV_A:(h + 1) * DV_A] = _unit_rms(o).astype(BF16) * og

    ti = lax.broadcasted_iota(jnp.int32, (l_b, l_b), 0)
    si = lax.broadcasted_iota(jnp.int32, (l_b, l_b), 1)
    rel = (ti - si).astype(F32)
    tcol = lax.broadcasted_iota(jnp.int32, (l_b, 1), 0).astype(F32)
    decays = [jnp.where(ti >= si, jnp.exp2(jnp.minimum(rel * lg, 0.0)), 0.0) for lg in LOG2_GAMMA]
    q_scale = [jnp.exp2((tcol + 1.0) * lg) for lg in LOG2_GAMMA]
    k_scale = [jnp.exp2((l_b - 1.0 - tcol) * lg) for lg in LOG2_GAMMA]

    def retention_head(c, h):
        rows = pl.ds(c * l_b, l_b)
        qb = proj_ref[rows, OFF_QB + h * DK_B:OFF_QB + (h + 1) * DK_B]
        kb = proj_ref[rows, OFF_KB + h * DK_B:OFF_KB + (h + 1) * DK_B]
        v = proj_ref[rows, OFF_VB + h * DV_B:OFF_VB + (h + 1) * DV_B]
        sg = proj_ref[rows, OFF_SGB + h * DV_B:OFF_SGB + (h + 1) * DV_B]
        scores = _dot_nt(qb, kb) * decays[h]
        s_prev = sb_s[h]
        o = (_dot(scores.astype(BF16), v)
             + _dot((qb.astype(F32) * q_scale[h]).astype(BF16), s_prev.astype(BF16)))
        k_tail = (kb.astype(F32) * k_scale[h]).astype(BF16)
        sb_s[h] = s_prev * (2.0 ** (l_b * LOG2_GAMMA[h])) + _dot_tn(k_tail, v)
        ob_ref[rows, h * DV_B:(h + 1) * DV_B] = _unit_rms(o).astype(BF16) * sg

    n_a = tb // l_a
    prep = stage1_prep(0, 0)
    for h in range(H_A):
        stage1_head(prep, h, 0)
    if n_a == 1:
        for h in range(H_A):
            stage2_head(0, h, 0)
            retention_head(0, h)
    else:
        for s in range(n_a // 2):
            prep = stage1_prep(2 * s + 1, 1)
            for h in range(H_A):
                stage1_head(prep, h, 1)
                stage2_head(2 * s, h, 0)
            prep = stage1_prep(2 * s + 2, 0) if 2 * s + 2 < n_a else None
            for h in range(H_A):
                if prep is not None:
                    stage1_head(prep, h, 0)
                stage2_head(2 * s + 1, h, 1)
                retention_head(s, h)

    @pl.when(t == pl.num_programs(1) - 1)
    def _():
        sa_out_ref[...] = sa_s[...]
        sb_out_ref[...] = sb_s[...]


def _diag_selector(l_a):
    s_of_row = jnp.arange(SUB * DK_A, dtype=jnp.int32) // DK_A
    col = jnp.arange(l_a, dtype=jnp.int32)
    return (s_of_row[:, None] == (col[None, :] % SUB)).astype(BF16)


def _recurrences(proj, g_hi, g_lo, s_a0, s_b0, sa_all, sb_all, layer, tb):
    bsz, t_len, _ = proj.shape
    l_a = min(64, t_len)
    l_b = min(128, t_len)
    state_spec = pl.BlockSpec((None, H_A, DK_A, DV_A), lambda b, t: (b, 0, 0, 0))
    state_out_spec = pl.BlockSpec((None, None, H_A, DK_A, DV_A), lambda b, t: (layer, b, 0, 0, 0))
    hbm = pl.BlockSpec(memory_space=pl.ANY)
    act_spec = pl.BlockSpec((None, tb, D_MODEL), lambda b, t: (b, t, 0))
    g_spec = pl.BlockSpec((None, tb, H_A * DK_A), lambda b, t: (b, t, 0))
    return pl.pallas_call(
        functools.partial(_recurrence_kernel, tb=tb, l_a=l_a, l_b=l_b),
        grid=(bsz, t_len // tb),
        in_specs=[
            pl.BlockSpec((None, tb, REC_COLS), lambda b, t: (b, t, 0)),
            g_spec, g_spec,
            pl.BlockSpec((SUB * DK_A, l_a), lambda b, t: (0, 0)),
            state_spec, state_spec, hbm, hbm,
        ],
        out_specs=[act_spec, act_spec, state_out_spec, state_out_spec],
        out_shape=[
            jax.ShapeDtypeStruct((bsz, t_len, D_MODEL), BF16),
            jax.ShapeDtypeStruct((bsz, t_len, D_MODEL), BF16),
            jax.ShapeDtypeStruct(sa_all.shape, F32),
            jax.ShapeDtypeStruct(sb_all.shape, F32),
        ],
        input_output_aliases={6: 2, 7: 3},
        scratch_shapes=[pltpu.VMEM((H_A, DK_A, DV_A), F32), pltpu.VMEM((H_B, DK_B, DV_B), F32),
                        pltpu.VMEM((H_A * l_a, SUB * DK_A), BF16),
                        pltpu.VMEM((2, H_A, l_a, l_a), BF16),
                        pltpu.VMEM((2, H_A, l_a, DK_A), BF16),
                        pltpu.VMEM((2, H_A, DK_A, DK_A), F32),
                        pltpu.VMEM((2, l_a, H_A * DK_A), F32)],
        compiler_params=_params(("parallel", "arbitrary")),
        name="recurrences",
    )(proj, g_hi, g_lo, _diag_selector(l_a), s_a0, s_b0, sa_all, sb_all)


def _merge_kernel(*refs, with_router):
    if with_router:
        (oa_ref, ob_ref, mg_ref, x_ref, mod_ref, gains_ref, wba_ref, wbb_ref, wo_ref,
         wr_ref, br_ref, lt_ref, xo_ref, h2_ref, route_ref, cnt_ref, run_s) = refs
    else:
        (oa_ref, ob_ref, mg_ref, x_ref, mod_ref, gains_ref, wba_ref, wbb_ref, wo_ref,
         xo_ref, h2_ref) = refs
    d = D_MODEL
    tm = x_ref.shape[0]
    halves = [pl.ds(0, tm // 2), pl.ds(tm // 2, tm // 2)] if tm >= 256 else [pl.ds(0, tm)]
    merged = []
    for rows in halves:
        y_a = _dot(oa_ref[rows, :], wba_ref[...])
        y_b = _dot(ob_ref[rows, :], wbb_ref[...])
        merged.append((mg_ref[rows, :d].astype(F32) * y_a
                       + mg_ref[rows, d:].astype(F32) * y_b).astype(BF16))
    m_outs = [_dot(m, wo_ref[...]) for m in merged]
    if with_router:
        w_hi, w_lo = _split_bf16(wr_ref[...])
    logit_parts = []
    for rows, m_out in zip(halves, m_outs):
        x = x_ref[rows, :] + mod_ref[2] * _rms(m_out, gains_ref[1:2])
        xo_ref[rows, :] = x
        h2 = _rms(x, gains_ref[2:3]) * (1.0 + mod_ref[4]) + mod_ref[3]
        if not with_router:
            h2_ref[rows, :] = h2.astype(BF16)
            continue
        h2_ref[rows, :] = h2
        h_hi, h_lo = _split_bf16(h2)
        logit_parts.append(_dot_nt(w_hi, h_hi) + _dot_nt(w_lo, h_hi) + _dot_nt(w_hi, h_lo))
    if not with_router:
        return

    @pl.when((pl.program_id(0) == 0) & (pl.program_id(1) == 0))
    def _():
        run_s[...] = jnp.zeros_like(run_s)

    logits = jnp.concatenate(logit_parts, axis=1) + br_ref[...]
    exp_id = lax.broadcasted_iota(jnp.int32, logits.shape, 0)
    m1 = jnp.max(logits, axis=0, keepdims=True)
    i1 = jnp.min(jnp.where(logits == m1, exp_id, N_EXPERTS), axis=0, keepdims=True)
    rest = jnp.where(exp_id == i1, -jnp.inf, logits)
    m2 = jnp.max(rest, axis=0, keepdims=True)
    i2 = jnp.min(jnp.where(rest == m2, exp_id, N_EXPERTS), axis=0, keepdims=True)
    e2 = jnp.exp(m2 - m1)
    w1 = 1.0 / (1.0 + e2)
    w2 = e2 / (1.0 + e2)
    chosen = ((exp_id == i1) | (exp_id == i2)).astype(F32)
    before = run_s[...] + _dot(chosen.astype(BF16), lt_ref[...])
    r1 = jnp.sum(jnp.where(exp_id == i1, before, 0.0), axis=0, keepdims=True)
    r2 = jnp.sum(jnp.where(exp_id == i2, before, 0.0), axis=0, keepdims=True)
    run_s[...] = run_s[...] + jnp.sum(chosen, axis=1, keepdims=True)
    cnt_ref[...] = run_s[...]
    route = jnp.zeros(logits.shape, F32)
    for j, val in enumerate((i1.astype(F32), i2.astype(F32), r1, r2, w1, w2)):
        route = jnp.where(exp_id == j, val, route)
    route_ref[...] = route


def _merge(o_a, o_b, proj, x, mod_l, gains_l, wba, wbb, wo, router, tm):
    bsz, t_len, d = x.shape
    with_router = router is not None
    tok = pl.BlockSpec((None, tm, d), lambda b, i: (b, i, 0))
    in_specs = [
        tok, tok,
        pl.BlockSpec((None, tm, 2 * d), lambda b, i: (b, i, OFF_MGA // (2 * d))),
        tok,
        pl.BlockSpec((N_MOD, None, 1, d), lambda b, i: (0, b, 0, 0)),
        pl.BlockSpec((4, d), lambda b, i: (0, 0)),
        _resident((d, d), lambda b, i: (0, 0)),
        _resident((d, d), lambda b, i: (0, 0)),
        _resident((d, d), lambda b, i: (0, 0)),
    ]
    args = [o_a, o_b, proj, x, mod_l, gains_l, wba, wbb, wo]
    if not with_router:
        return pl.pallas_call(
            functools.partial(_merge_kernel, with_router=False),
            grid=(bsz, t_len // tm),
            in_specs=in_specs,
            out_specs=[tok, tok],
            out_shape=[jax.ShapeDtypeStruct((bsz, t_len, d), F32),
                       jax.ShapeDtypeStruct((bsz, t_len, d), BF16)],
            compiler_params=_params(("parallel", "parallel")),
            name="merge",
        )(*args)
    w_r, b_r = router
    earlier = jnp.triu(jnp.ones((tm, tm), BF16), 1)
    in_specs += [pl.BlockSpec((N_EXPERTS, d), lambda b, i: (0, 0)),
                 pl.BlockSpec((N_EXPERTS, 1), lambda b, i: (0, 0)),
                 pl.BlockSpec((tm, tm), lambda b, i: (0, 0))]
    args += [w_r.T, b_r.reshape(N_EXPERTS, 1), earlier]
    nt = t_len // tm
    return pl.pallas_call(
        functools.partial(_merge_kernel, with_router=True),
        grid=(bsz, nt),
        in_specs=in_specs,
        out_specs=[tok, tok,
                   pl.BlockSpec((None, None, N_EXPERTS, tm), lambda b, i: (b, i, 0, 0)),
                   pl.BlockSpec((N_EXPERTS, 1), lambda b, i: (0, 0))],
        out_shape=[jax.ShapeDtypeStruct((bsz, t_len, d), F32),
                   jax.ShapeDtypeStruct((bsz, t_len, d), F32),
                   jax.ShapeDtypeStruct((bsz, nt, N_EXPERTS, tm), F32),
                   jax.ShapeDtypeStruct((N_EXPERTS, 1), F32)],
        scratch_shapes=[pltpu.VMEM((N_EXPERTS, 1), F32)],
        compiler_params=_params(("arbitrary", "arbitrary")),
        name="merge_router",
    )(*args)


def _swiglu_hidden(hb, wg_ref, wu_ref, a_s):
    for c in range(FFN_DIM // MXU_N):
        sl = slice(c * MXU_N, (c + 1) * MXU_N)
        gate = _dot(hb, wg_ref[:, sl])
        up = _dot(hb, wu_ref[:, sl])
        a_s[:, sl] = (gate * _sigmoid(gate) * up).astype(BF16)


def _ffn_dense_kernel(h_ref, wg_ref, wu_ref, wd_ref, x_ref, mod_ref, gain_ref, o_ref, a_s):
    _swiglu_hidden(h_ref[...], wg_ref, wu_ref, a_s)
    y = _dot(a_s[...], wd_ref[...])
    o_ref[...] = x_ref[...] + mod_ref[0] * _rms(y, gain_ref[...])


def _merge_ffn_kernel(oa_ref, ob_ref, mg_ref, x_ref, mod_ref, gains_ref, wba_ref, wbb_ref, wo_ref,
                      wg_ref, wu_ref, wd_ref, o_ref, x_s, h2_s, a_s):
    d = D_MODEL
    tm = x_ref.shape[0]
    halves = [pl.ds(0, tm // 2), pl.ds(tm // 2, tm // 2)] if tm >= 256 else [pl.ds(0, tm)]
    merged = []
    for rows in halves:
        y_a = _dot(oa_ref[rows, :], wba_ref[...])
        y_b = _dot(ob_ref[rows, :], wbb_ref[...])
        merged.append((mg_ref[rows, :d].astype(F32) * y_a
                       + mg_ref[rows, d:].astype(F32) * y_b).astype(BF16))
    m_outs = [_dot(m, wo_ref[...]) for m in merged]
    for rows, m_out in zip(halves, m_outs):
        x = x_ref[rows, :] + mod_ref[2] * _rms(m_out, gains_ref[1:2])
        x_s[rows, :] = x
        h2_s[rows, :] = (_rms(x, gains_ref[2:3]) * (1.0 + mod_ref[4]) + mod_ref[3]).astype(BF16)
    _swiglu_hidden(h2_s[...], wg_ref, wu_ref, a_s)
    y = _dot(a_s[...], wd_ref[...])
    o_ref[...] = x_s[...] + mod_ref[5] * _rms(y, gains_ref[3:4])


def _merge_ffn(o_a, o_b, proj, x, mod_l, gains_l, wba, wbb, wo, wg, wu, wd, tm):
    bsz, t_len, d = x.shape
    tok = pl.BlockSpec((None, tm, d), lambda b, i: (b, i, 0))
    return pl.pallas_call(
        _merge_ffn_kernel,
        grid=(bsz, t_len // tm),
        in_specs=[
            tok, tok,
            pl.BlockSpec((None, tm, 2 * d), lambda b, i: (b, i, OFF_MGA // (2 * d))),
            tok,
            pl.BlockSpec((N_MOD, None, 1, d), lambda b, i: (0, b, 0, 0)),
            pl.BlockSpec((4, d), lambda b, i: (0, 0)),
            _resident((d, d), lambda b, i: (0, 0)),
            _resident((d, d), lambda b, i: (0, 0)),
            _resident((d, d), lambda b, i: (0, 0)),
            _resident((d, FFN_DIM), lambda b, i: (0, 0)),
            _resident((d, FFN_DIM), lambda b, i: (0, 0)),
            _resident((FFN_DIM, d), lambda b, i: (0, 0)),
        ],
        out_specs=tok,
        out_shape=jax.ShapeDtypeStruct((bsz, t_len, d), F32),
        scratch_shapes=[pltpu.VMEM((tm, d), F32), pltpu.VMEM((tm, d), BF16),
                        pltpu.VMEM((tm, FFN_DIM), BF16)],
        compiler_params=_params(("parallel", "parallel")),
        name="merge_ffn_dense",
    )(o_a, o_b, proj, x, mod_l, gains_l, wba, wbb, wo, wg, wu, wd)


def _ffn_dense(h2, wg, wu, wd, x, mod_l, gain, tm):
    bsz, t_len, d = x.shape
    tok = pl.BlockSpec((None, tm, d), lambda b, i: (b, i, 0))
    return pl.pallas_call(
        _ffn_dense_kernel,
        grid=(bsz, t_len // tm),
        in_specs=[
            tok,
            _resident((d, FFN_DIM), lambda b, i: (0, 0)),
            _resident((d, FFN_DIM), lambda b, i: (0, 0)),
            _resident((FFN_DIM, d), lambda b, i: (0, 0)),
            tok,
            pl.BlockSpec((1, None, 1, d), lambda b, i: (N_MOD - 1, b, 0, 0)),
            pl.BlockSpec((1, d), lambda b, i: (0, 0)),
        ],
        out_specs=tok,
        out_shape=jax.ShapeDtypeStruct((bsz, t_len, d), F32),
        scratch_shapes=[pltpu.VMEM((tm, FFN_DIM), BF16)],
        compiler_params=_params(("parallel", "parallel")),
        name="ffn_dense",
    )(h2, wg, wu, wd, x, mod_l, gain)


def _ffn_grouped_kernel(te_ref, nu_ref, x_ref, wg_ref, wu_ref, wd_ref, y_ref, a_s):
    del te_ref

    @pl.when(pl.program_id(0) < nu_ref[0])
    def _():
        tmg = x_ref.shape[0]
        x = x_ref[...].reshape(tmg, D_MODEL)
        _swiglu_hidden(x.astype(BF16), wg_ref, wu_ref, a_s)
        y_ref[...] = _dot(a_s[...], wd_ref[...]).reshape(y_ref.shape)


def _ffn_grouped(x_sorted, tile_expert, n_used, wg, wu, wd, tmg):
    n_slots = x_sorted.shape[0]
    d = D_MODEL
    n_tiles = n_slots // tmg

    def row_map(i, te, nu):
        return (jnp.minimum(i, nu[0] - 1), 0, 0)

    def w_map(i, te, nu):
        return (te[i], 0, 0)

    return pl.pallas_call(
        _ffn_grouped_kernel,
        grid_spec=pltpu.PrefetchScalarGridSpec(
            num_scalar_prefetch=2,
            grid=(n_tiles,),
            in_specs=[
                pl.BlockSpec((tmg,) + ROW_TILE, row_map),
                pl.BlockSpec((None, d, FFN_DIM), w_map),
                pl.BlockSpec((None, d, FFN_DIM), w_map),
                pl.BlockSpec((None, FFN_DIM, d), w_map),
            ],
            out_specs=pl.BlockSpec((tmg,) + ROW_TILE, row_map),
            scratch_shapes=[pltpu.VMEM((tmg, FFN_DIM), BF16)],
        ),
        out_shape=jax.ShapeDtypeStruct((n_slots,) + ROW_TILE, F32),
        compiler_params=_params(("arbitrary",)),
        name="ffn_grouped",
    )(tile_expert, n_used, x_sorted, wg, wu, wd)


def _row_copy(src_ref, dst_ref, sem):
    return pltpu.make_async_copy(src_ref, dst_ref, sem)


def _start_rows(copies, n_rows):
    def issue(r, carry):
        for i, cp in enumerate(copies(r)):
            cp.start(priority=i % 2)
        return carry

    lax.fori_loop(0, n_rows, issue, 0, unroll=8)


def _wait_rows(copies, n_rows):
    def drain(r, carry):
        for cp in copies(r):
            cp.wait()
        return carry

    lax.fori_loop(0, n_rows, drain, 0, unroll=8)


def _scatter_kernel(s1_ref, s2_ref, p1_ref, p2_ref, ztile_ref, h_ref, xs_ref,
                    rows_s, zero_s, sems, zsem):
    tm = h_ref.shape[0]
    tmg = zero_s.shape[0]
    i = pl.program_id(0)
    par = i % 2

    @pl.when(i == 0)
    def _():
        zero_s[...] = jnp.zeros_like(zero_s)

        def zero_copy(e):
            return pltpu.make_async_copy(zero_s, xs_ref.at[pl.ds(ztile_ref[e], tmg)], zsem)

        for e in range(N_EXPERTS):
            pl.when(ztile_ref[e] >= 0)(lambda e=e: zero_copy(e).start())
        for e in range(N_EXPERTS):
            pl.when(ztile_ref[e] >= 0)(lambda e=e: zero_copy(e).wait())

    def row_copies(slot_a, slot_b, p):
        def copies(r):
            src = rows_s.at[p, pl.ds(r, 1)]
            return (_row_copy(src, xs_ref.at[pl.ds(slot_a[0, r], 1)], sems.at[p]),
                    _row_copy(src, xs_ref.at[pl.ds(slot_b[0, r], 1)], sems.at[p]))
        return copies

    rows_s[par] = h_ref[...].reshape(rows_s.shape[1:])
    _start_rows(row_copies(s1_ref, s2_ref, par), tm)
    pl.when(i > 0)(lambda: _wait_rows(row_copies(p1_ref, p2_ref, 1 - par), tm))
    pl.when(i == pl.num_programs(0) - 1)(lambda: _wait_rows(row_copies(s1_ref, s2_ref, par), tm))


def _scatter_rows(h2, slot1, slot2, zero_tile_row, n_slots, tm, tmg):
    bsz, t_len, d = h2.shape
    n = bsz * t_len // tm

    def slot_spec(shift):
        return pl.BlockSpec((None, 1, tm), lambda i: (jnp.clip(i + shift, 0, n - 1), 0, 0),
                            memory_space=pltpu.SMEM)

    return pl.pallas_call(
        _scatter_kernel,
        grid=(n,),
        in_specs=[slot_spec(0), slot_spec(0), slot_spec(-1), slot_spec(-1),
                  pl.BlockSpec(memory_space=pltpu.SMEM),
                  pl.BlockSpec((tm, d), lambda i: (i, 0))],
        out_specs=pl.BlockSpec(memory_space=pl.ANY),
        out_shape=jax.ShapeDtypeStruct((n_slots,) + ROW_TILE, F32),
        scratch_shapes=[pltpu.VMEM((2, tm) + ROW_TILE, F32), pltpu.VMEM((tmg,) + ROW_TILE, F32),
                        pltpu.SemaphoreType.DMA((2,)), pltpu.SemaphoreType.DMA],
        compiler_params=_params(("arbitrary",)),
        name="scatter_rows",
    )(slot1, slot2, slot1, slot2, zero_tile_row, h2.reshape(bsz * t_len, d))


def _combine_kernel(s1_ref, s2_ref, n1_ref, n2_ref, route_ref, x_ref, mod_ref, gain_ref, y_ref,
                    o_ref, buf1, buf2, sems):
    tm, d = x_ref.shape
    i = pl.program_id(0)
    par = i % 2

    def row_copies(slot_a, slot_b, p):
        def copies(r):
            return (_row_copy(y_ref.at[pl.ds(slot_a[0, r], 1)], buf1.at[p, pl.ds(r, 1)], sems.at[p]),
                    _row_copy(y_ref.at[pl.ds(slot_b[0, r], 1)], buf2.at[p, pl.ds(r, 1)], sems.at[p]))
        return copies

    pl.when(i == 0)(lambda: _start_rows(row_copies(s1_ref, s2_ref, par), tm))
    pl.when(i + 1 < pl.num_programs(0))(
        lambda: _start_rows(row_copies(n1_ref, n2_ref, 1 - par), tm))
    _wait_rows(row_copies(s1_ref, s2_ref, par), tm)
    weights = route_ref[...].T
    f_out = (weights[:, 4:5] * buf1[par].reshape(tm, d)
             + weights[:, 5:6] * buf2[par].reshape(tm, d))
    o_ref[...] = x_ref[...] + mod_ref[0] * _rms(f_out, gain_ref[...])


def _combine_rows(y_sorted, slot1, slot2, route, x, mod_l, gain, tm):
    bsz, t_len, d = x.shape
    nt = t_len // tm
    n = bsz * nt

    def slot_spec(shift):
        return pl.BlockSpec((None, 1, tm), lambda i: (jnp.minimum(i + shift, n - 1), 0, 0),
                            memory_space=pltpu.SMEM)

    tok = pl.BlockSpec((None, tm, d), lambda i: (i // nt, i % nt, 0))
    return pl.pallas_call(
        _combine_kernel,
        grid=(n,),
        in_specs=[slot_spec(0), slot_spec(0), slot_spec(1), slot_spec(1),
                  pl.BlockSpec((None, None, N_EXPERTS, tm), lambda i: (i // nt, i % nt, 0, 0)),
                  tok,
                  pl.BlockSpec((1, None, 1, d), lambda i: (N_MOD - 1, i // nt, 0, 0)),
                  pl.BlockSpec((1, d), lambda i: (0, 0)),
                  pl.BlockSpec(memory_space=pl.ANY)],
        out_specs=tok,
        out_shape=jax.ShapeDtypeStruct((bsz, t_len, d), F32),
        scratch_shapes=[pltpu.VMEM((2, tm) + ROW_TILE, F32), pltpu.VMEM((2, tm) + ROW_TILE, F32),
                        pltpu.SemaphoreType.DMA((2,))],
        compiler_params=_params(("arbitrary",)),
        name="combine_rows",
    )(slot1, slot2, slot1, slot2, route, x, mod_l, gain, y_sorted)


def _moe_ffn(h2, route, counts, wg, wu, wd, first_expert, x, mod_l, gain, tm, tmg):
    bsz, t_len, d = x.shape
    n_tok = bsz * t_len
    n_tiles = (2 * n_tok) // tmg + N_EXPERTS
    e1, e2, r1, r2 = (route[:, :, j, :].astype(jnp.int32).reshape(n_tok) for j in range(4))
    cnt = counts[:, 0].astype(jnp.int32)
    tiles_per = (cnt + tmg - 1) // tmg
    tile_end = jnp.cumsum(tiles_per)
    offs = (tile_end - tiles_per) * tmg
    n_used = tile_end[-1:]
    tile_ids = jnp.arange(n_tiles, dtype=jnp.int32)
    tile_expert = jnp.sum(tile_ids[:, None] >= tile_end[None, :], axis=1).astype(jnp.int32)
    last_expert = jnp.sum(n_used - 1 >= tile_end).astype(jnp.int32)
    tile_expert = jnp.where(tile_ids < n_used, tile_expert, last_expert)
    zero_tile_row = jnp.where(cnt > 0, (tile_end - 1) * tmg, -1).astype(jnp.int32)
    nt = t_len // tm
    slot1 = (offs[e1] + r1).reshape(bsz * nt, 1, tm)
    slot2 = (offs[e2] + r2).reshape(bsz * nt, 1, tm)
    x_sorted = _scatter_rows(h2, slot1, slot2, zero_tile_row, n_tiles * tmg, tm, tmg)
    y_sorted = _ffn_grouped(x_sorted, tile_expert + first_expert, n_used, wg, wu, wd, tmg)
    return _combine_rows(y_sorted, slot1, slot2, route, x, mod_l, gain, tm)


def _rope_tables(pos):
    half = DK_B // 2
    inv = ROPE_BASE ** (-jnp.arange(half, dtype=F32) / half)
    ang = pos.astype(F32)[:, None] * inv[None, :]
    cos, sin = jnp.cos(ang), jnp.sin(ang)
    return jnp.concatenate([cos, cos], axis=-1), jnp.concatenate([-sin, sin], axis=-1)


def _trunk(x, mod, pos, s_a_in, s_b_in, lower_bounds, wts, tm, tb, tmg):
    depth = mod.shape[0]
    bsz = x.shape[0]
    cos_t, sin_t = _rope_tables(pos)
    sa_all = jnp.zeros((depth, bsz, H_A, DK_A, DV_A), F32)
    sb_all = jnp.zeros((depth, bsz, H_B, DK_B, DV_B), F32)
    for l in range(depth):
        mod_l = mod[l]
        gains_l = wts["norm_gains"][l]
        if s_a_in is None:
            s_a = jnp.zeros((bsz, H_A, DK_A, DV_A), F32)
            s_b = jnp.zeros((bsz, H_B, DK_B, DV_B), F32)
        else:
            s_a, s_b = s_a_in[l], s_b_in[l]
        proj, g_hi, g_lo = _in_projection(
            x, mod_l, gains_l[0:1], lower_bounds[l:l + 1],
            wts["g_norm_a"][l].reshape(1, H_A * DV_A), wts["g_norm_b"][l].reshape(1, H_B * DV_B),
            cos_t, sin_t, wts["w_in"][l], tm)
        o_a, o_b, sa_all, sb_all = _recurrences(proj, g_hi, g_lo, s_a, s_b, sa_all, sb_all, l, tb)
        j = l // 2
        if l % 2 == 0:
            x = _merge_ffn(o_a, o_b, proj, x, mod_l, gains_l, wts["w_branch_a"][l],
                           wts["w_branch_b"][l], wts["w_out"][l], wts["w_gate_dense"][j],
                           wts["w_up_dense"][j], wts["w_down_dense"][j], tm)
        else:
            router = (wts["w_router"][j], wts["b_router"][j])
            x, h2, route, counts = _merge(o_a, o_b, proj, x, mod_l, gains_l, wts["w_branch_a"][l],
                                          wts["w_branch_b"][l], wts["w_out"][l], router, tm)
            x = _moe_ffn(h2, route, counts, wts["w_gate_exp"], wts["w_up_exp"],
                         wts["w_down_exp"], j * N_EXPERTS, x, mod_l, gains_l[3:4], tm, tmg)
    return x, sa_all, sb_all


def kernel(x_prompt, x_sample, state_hgrn, state_ret, c_prompt, c_sample, w_mod, b_mod, norm_gains,
           w_in, lb_logits, g_norm_a, g_norm_b, w_branch_a, w_branch_b, w_out, w_gate_dense,
           w_up_dense, w_down_dense, w_router, b_router, w_gate_exp, w_up_exp, w_down_exp):
    n_prompt = x_prompt.shape[0]
    past_len = 1024
    wts = dict(
        norm_gains=norm_gains, g_norm_a=g_norm_a, g_norm_b=g_norm_b,
        w_router=w_router, b_router=b_router,
        w_in=w_in.astype(BF16), w_branch_a=w_branch_a.astype(BF16),
        w_branch_b=w_branch_b.astype(BF16), w_out=w_out.astype(BF16),
        w_gate_dense=w_gate_dense.astype(BF16), w_up_dense=w_up_dense.astype(BF16),
        w_down_dense=w_down_dense.astype(BF16),
        w_gate_exp=w_gate_exp.astype(BF16).reshape((-1,) + w_gate_exp.shape[2:]),
        w_up_exp=w_up_exp.astype(BF16).reshape((-1,) + w_up_exp.shape[2:]),
        w_down_exp=w_down_exp.astype(BF16).reshape((-1,) + w_down_exp.shape[2:]))
    lb_p = jax.nn.softmax(lb_logits.astype(F32), axis=0)
    lower_bounds = jnp.clip(jnp.cumsum(lb_p, axis=0) - lb_p[:1], 0.0, 1.0)
    c_all = jnp.concatenate([c_prompt, c_sample], axis=0)
    mod = _modulation(c_all, w_mod, b_mod)
    mod = mod.reshape(mod.shape[:3] + (1, D_MODEL))
    pos_prompt = jnp.arange(x_prompt.shape[1], dtype=jnp.int32)
    pos_sample = past_len + jnp.arange(x_sample.shape[1], dtype=jnp.int32)
    t_p, t_s = x_prompt.shape[1], x_sample.shape[1]
    y_p, hg_p, rt_p = _trunk(x_prompt, mod[:, :, :n_prompt], pos_prompt, None, None,
                             lower_bounds, wts, tm=min(512, t_p), tb=min(1024, t_p), tmg=512)
    y_s, hg_s, rt_s = _trunk(x_sample, mod[:, :, n_prompt:], pos_sample, state_hgrn, state_ret,
                             lower_bounds, wts, tm=min(512, t_s), tb=min(512, t_s), tmg=64)
    return (y_p, y_s, hg_p, rt_p, hg_s, rt_s)
```
